```python
import math
import jax, jax.numpy as jnp
from jax import lax
import numpy as np

D_MODEL = 1024
BATCH = 8
SEQ = 8192
DEPTH = 4

CHUNK = 64
Q_BLOCK = 128
KEY_BLOCK = 128
N_EVEN = (DEPTH + 1) // 2
N_ODD = DEPTH // 2
SB_HEADS = 4
SB_HEAD_DIM = 128
SB_WIDTH = SB_HEADS * SB_HEAD_DIM
DIFF_HEADS = 4
DIFF_HEAD_DIM = 64
DIFF_WIDTH = DIFF_HEADS * 2 * DIFF_HEAD_DIM
IN_PROJ_WIDTH = 3 * SB_WIDTH + 3 * DIFF_WIDTH
ROPE_THETA = 10000.0
CONV_CHANNELS = D_MODEL
CONV_WIDTH = 31
N_MEM = 256
XA_HEADS = 4
XA_HEAD_DIM = 128
XA_WIDTH = XA_HEADS * XA_HEAD_DIM
D_FF = 2816
N_EXPERTS = 8
TOP_K = 2
D_FF_EXPERT = 1792
MOE_BLOCK = 1024
DEEPNORM_ALPHA = (2 * DEPTH) ** 0.25
DEEPNORM_BETA = (8 * DEPTH) ** -0.25
LN_EPS = 1e-5

kernel_name = "hybrid_stickbreak_diffattn_conformer_moe_deepnorm"


def _layer_norm(x, g, b):
    xf = x.astype(jnp.float32)
    mu = jnp.mean(xf, axis=-1, keepdims=True)
    var = jnp.mean(jnp.square(xf - mu), axis=-1, keepdims=True)
    return ((xf - mu) * lax.rsqrt(var + LN_EPS) * g + b).astype(x.dtype)


def _rms_norm(x, g):
    xf = x.astype(jnp.float32)
    return (xf * lax.rsqrt(jnp.mean(jnp.square(xf), axis=-1, keepdims=True) + LN_EPS) * g).astype(x.dtype)


def _rope_tables(seq, dim):
    inv = ROPE_THETA ** (-jnp.arange(0, dim, 2, dtype=jnp.float32) / dim)
    ang = jnp.arange(seq, dtype=jnp.float32)[:, None] * inv[None, :]
    return jnp.cos(ang), jnp.sin(ang)


def _apply_rope(x, cos, sin):
    x1, x2 = jnp.split(x.astype(jnp.float32), 2, axis=-1)
    c = cos[None, :, None, :]
    s = sin[None, :, None, :]
    return jnp.concatenate([x1 * c - x2 * s, x2 * c + x1 * s], axis=-1).astype(x.dtype)


def _stick_breaking_attention(q, k, v):
    b, s = q.shape[0], q.shape[1]
    scale = SB_HEAD_DIM ** -0.5
    kb_idx = jnp.arange(KEY_BLOCK)
    upper = (kb_idx[:, None] >= kb_idx[None, :]).astype(jnp.float32)
    outs = []
    for t0 in range(0, s, Q_BLOCK):
        kend = t0 + Q_BLOCK
        nk = kend // KEY_BLOCK
        z = jnp.einsum('bqhd,bkhd->bhqk', q[:, t0:kend], k[:, :kend]).astype(jnp.float32) * scale
        strict = jnp.arange(kend)[None, :] < (t0 + jnp.arange(Q_BLOCK))[:, None]
        log_keep = jnp.where(strict, -jax.nn.softplus(z), 0.0)
        log_keep = log_keep.reshape(b, SB_HEADS, Q_BLOCK, nk, KEY_BLOCK)
        r_in = jnp.einsum('bhqcj,js->bhqcs', log_keep, upper, precision=lax.Precision.HIGHEST)
        c_idx = jnp.arange(nk)
        later = (c_idx[:, None] > c_idx[None, :]).astype(jnp.float32)
        offs = jnp.einsum('bhqc,cd->bhqd', r_in[..., 0], later, precision=lax.Precision.HIGHEST)
        log_a = z + (r_in + offs[..., None]).reshape(b, SB_HEADS, Q_BLOCK, kend)
        w = jnp.where(strict, jnp.exp(log_a), 0.0)
        outs.append(jnp.einsum('bhqk,bkhd->bqhd', w.astype(v.dtype), v[:, :kend]))
    return jnp.concatenate(outs, axis=1)


def _differential_attention(q, k, v, lam, subln_g, lambda_init):
    b, s = q.shape[0], q.shape[1]
    scale = DIFF_HEAD_DIM ** -0.5
    outs = []
    for t0 in range(0, s, Q_BLOCK):
        kend = t0 + Q_BLOCK
        q_chunk = (t0 + jnp.arange(Q_BLOCK)) // CHUNK
        allowed = (jnp.arange(kend) // CHUNK)[None, :] <= q_chunk[:, None]
        z = jnp.einsum('bqhd,bkhd->bhqk', q[:, t0:kend], k[:, :kend]).astype(jnp.float32) * scale
        p = jax.nn.softmax(jnp.where(allowed, z, -jnp.inf), axis=-1)
        p = p.reshape(b, DIFF_HEADS, 2, Q_BLOCK, kend)
        w = p[:, :, 0] - lam * p[:, :, 1]
        outs.append(jnp.einsum('bhqk,bkhe->bqhe', w.astype(v.dtype), v[:, :kend]))
    o = jnp.concatenate(outs, axis=1)
    return _rms_norm(o, subln_g) * (1.0 - lambda_init)


def _attention_mixer(x, w_in, w_o, lq1, lk1, lq2, lk2, subln_g, cos, sin, lambda_init):
    b, s, _ = x.shape
    h = x @ w_in
    sq, sk, sv, dq, dk, dv = jnp.split(
        h, [SB_WIDTH, 2 * SB_WIDTH, 3 * SB_WIDTH, 3 * SB_WIDTH + DIFF_WIDTH, 3 * SB_WIDTH + 2 * DIFF_WIDTH], axis=-1)
    sb_shape = (b, s, SB_HEADS, SB_HEAD_DIM)
    o_sb = _stick_breaking_attention(sq.reshape(sb_shape), sk.reshape(sb_shape), sv.reshape(sb_shape))
    lam = (jnp.exp(jnp.sum(lq1 * lk1).astype(jnp.float32))
           - jnp.exp(jnp.sum(lq2 * lk2).astype(jnp.float32)) + lambda_init)
    qk_shape = (b, s, 2 * DIFF_HEADS, DIFF_HEAD_DIM)
    o_diff = _differential_attention(
        _apply_rope(dq.reshape(qk_shape), cos, sin),
        _apply_rope(dk.reshape(qk_shape), cos, sin),
        dv.reshape(b, s, DIFF_HEADS, 2 * DIFF_HEAD_DIM), lam, subln_g, lambda_init)
    o = jnp.concatenate([o_sb.reshape(b, s, SB_WIDTH), o_diff.reshape(b, s, DIFF_WIDTH)], axis=-1)
    return o @ w_o


def _conv_module(x, w_pw1, b_pw1, dw_w, dw_b, ln_g, ln_b, w_pw2, b_pw2):
    a, g = jnp.split(x @ w_pw1 + b_pw1, 2, axis=-1)
    h = a * jax.nn.sigmoid(g)
    h = lax.conv_general_dilated(
        h, dw_w[:, None, :], window_strides=(1,), padding=[(CONV_WIDTH - 1, 0)],
        dimension_numbers=('NWC', 'WIO', 'NWC'), feature_group_count=CONV_CHANNELS) + dw_b
    h = jax.nn.silu(_layer_norm(h, ln_g, ln_b))
    return h @ w_pw2 + b_pw2


def _memory_cross_attention(x, mem, wq, wkv, wo):
    b, s, _ = x.shape
    m = mem.shape[1]
    q = (x @ wq).reshape(b, s, XA_HEADS, XA_HEAD_DIM)
    k, v = jnp.split(mem @ wkv, 2, axis=-1)
    k = k.reshape(b, m, XA_HEADS, XA_HEAD_DIM)
    v = v.reshape(b, m, XA_HEADS, XA_HEAD_DIM)
    z = jnp.einsum('bshd,bmhd->bhsm', q, k).astype(jnp.float32) * (XA_HEAD_DIM ** -0.5)
    p = jax.nn.softmax(z, axis=-1)
    o = jnp.einsum('bhsm,bmhd->bshd', p.astype(v.dtype), v).reshape(b, s, XA_WIDTH)
    return o @ wo


def _swiglu(x, w1, w3, w2):
    return (jax.nn.silu(x @ w1) * (x @ w3)) @ w2


def _moe_swiglu(x, router_w, router_b, w1, w3, w2):
    b, s, d = x.shape
    t = x.reshape(-1, d)
    n = t.shape[0]
    n_slots = n * TOP_K
    logits = (t @ router_w).astype(jnp.float32) + router_b
    top_val, top_idx = lax.top_k(logits, TOP_K)
    gates = jax.nn.softmax(top_val, axis=-1)
    e_flat = top_idx.reshape(-1).astype(jnp.int32)
    order = jnp.argsort(e_flat)
    e_sorted = e_flat[order]
    tok_sorted = (order // TOP_K).astype(jnp.int32)
    g_sorted = gates.reshape(-1)[order]
    counts = jnp.zeros((N_EXPERTS,), jnp.int32).at[e_flat].add(1)
    padded = (counts + MOE_BLOCK - 1) // MOE_BLOCK * MOE_BLOCK
    start = jnp.cumsum(counts) - counts
    pend = jnp.cumsum(padded)
    pstart = pend - padded
    dest = pstart[e_sorted] + jnp.arange(n_slots, dtype=jnp.int32) - start[e_sorted]
    n_blocks = -(-n_slots // MOE_BLOCK) + N_EXPERTS
    buf = jnp.zeros((n_blocks * MOE_BLOCK, d), t.dtype).at[dest].set(t[tok_sorted])
    block_start = jnp.arange(n_blocks, dtype=jnp.int32) * MOE_BLOCK
    block_e = jnp.minimum(jnp.sum((block_start[:, None] >= pend[None, :]).astype(jnp.int32), axis=1),
                          N_EXPERTS - 1)

    def expert_block(args):
        xb, e = args
        return _swiglu(xb, w1[e], w3[e], w2[e])

    yb = lax.map(expert_block, (buf.reshape(n_blocks, MOE_BLOCK, d), block_e))
    y = yb.reshape(-1, d)[dest] * g_sorted[:, None].astype(t.dtype)
    return jax.ops.segment_sum(y, tok_sorted, num_segments=n).reshape(b, s, d)


def setup_inputs(seed: int = 0) -> dict:
    key = jax.random.key(seed)
    ks = iter(jax.random.split(key, 40))

    def nrm(shape, scale):
        return jax.random.normal(next(ks), shape, jnp.float32) * scale

    def gain(shape):
        return 1.0 + nrm(shape, 0.02)

    d = D_MODEL
    return {
        "x": nrm((BATCH, SEQ, d), 1.0),
        "mem": nrm((BATCH, N_MEM, d), 1.0),
        "ln1_g": gain((DEPTH, d)), "ln1_b": nrm((DEPTH, d), 0.02),
        "lnx_g": gain((DEPTH, d)), "lnx_b": nrm((DEPTH, d), 0.02),
        "ln2_g": gain((DEPTH, d)), "ln2_b": nrm((DEPTH, d), 0.02),
        "xa_wq": nrm((DEPTH, d, XA_WIDTH), d ** -0.5),
        "xa_wkv": nrm((DEPTH, d, 2 * XA_WIDTH), d ** -0.5),
        "xa_wo": nrm((DEPTH, XA_WIDTH, d), XA_WIDTH ** -0.5 * DEEPNORM_BETA),
        "attn_w_in": nrm((N_EVEN, d, IN_PROJ_WIDTH), d ** -0.5),
        "attn_w_o": nrm((N_EVEN, SB_WIDTH + DIFF_WIDTH, d), (SB_WIDTH + DIFF_WIDTH) ** -0.5 * DEEPNORM_BETA),
        "diff_lq1": nrm((N_EVEN, DIFF_HEAD_DIM), 0.1),
        "diff_lk1": nrm((N_EVEN, DIFF_HEAD_DIM), 0.1),
        "diff_lq2": nrm((N_EVEN, DIFF_HEAD_DIM), 0.1),
        "diff_lk2": nrm((N_EVEN, DIFF_HEAD_DIM), 0.1),
        "diff_subln_g": gain((N_EVEN, 2 * DIFF_HEAD_DIM)),
        "ffn_w1": nrm((N_EVEN, d, D_FF), d ** -0.5),
        "ffn_w3": nrm((N_EVEN, d, D_FF), d ** -0.5),
        "ffn_w2": nrm((N_EVEN, D_FF, d), D_FF ** -0.5 * DEEPNORM_BETA),
        "conv_w_pw1": nrm((N_ODD, d, 2 * CONV_CHANNELS), d ** -0.5),
        "conv_b_pw1": nrm((N_ODD, 2 * CONV_CHANNELS), 0.02),
        "conv_dw_w": nrm((N_ODD, CONV_WIDTH, CONV_CHANNELS), CONV_WIDTH ** -0.5),
        "conv_dw_b": nrm((N_ODD, CONV_CHANNELS), 0.02),
        "conv_ln_g": gain((N_ODD, CONV_CHANNELS)),
        "conv_ln_b": nrm((N_ODD, CONV_CHANNELS), 0.02),
        "conv_w_pw2": nrm((N_ODD, CONV_CHANNELS, d), CONV_CHANNELS ** -0.5 * DEEPNORM_BETA),
        "conv_b_pw2": nrm((N_ODD, d), 0.02),
        "moe_router_w": nrm((N_ODD, d, N_EXPERTS), d ** -0.5),
        "moe_router_b": nrm((N_ODD, N_EXPERTS), 0.01),
        "moe_w1": nrm((N_ODD, N_EXPERTS, d, D_FF_EXPERT), d ** -0.5),
        "moe_w3": nrm((N_ODD, N_EXPERTS, d, D_FF_EXPERT), d ** -0.5),
        "moe_w2": nrm((N_ODD, N_EXPERTS, D_FF_EXPERT, d), D_FF_EXPERT ** -0.5 * DEEPNORM_BETA),
    }


def reference(x, mem, ln1_g, ln1_b, lnx_g, lnx_b, ln2_g, ln2_b, xa_wq, xa_wkv, xa_wo,
              attn_w_in, attn_w_o, diff_lq1, diff_lk1, diff_lq2, diff_lk2, diff_subln_g,
              ffn_w1, ffn_w3, ffn_w2,
              conv_w_pw1, conv_b_pw1, conv_dw_w, conv_dw_b, conv_ln_g, conv_ln_b, conv_w_pw2, conv_b_pw2,
              moe_router_w, moe_router_b, moe_w1, moe_w3, moe_w2):
    cos, sin = _rope_tables(x.shape[1], DIFF_HEAD_DIM)
    for i in range(DEPTH):
        j = i // 2
        if i % 2 == 0:
            lambda_init = 0.8 - 0.6 * math.exp(-0.3 * i)
            mix = _attention_mixer(x, attn_w_in[j], attn_w_o[j], diff_lq1[j], diff_lk1[j],
                                   diff_lq2[j], diff_lk2[j], diff_subln_g[j], cos, sin, lambda_init)
        else:
            mix = _conv_module(x, conv_w_pw1[j], conv_b_pw1[j], conv_dw_w[j], conv_dw_b[j],
                               conv_ln_g[j], conv_ln_b[j], conv_w_pw2[j], conv_b_pw2[j])
        x = _layer_norm(DEEPNORM_ALPHA * x + mix, ln1_g[i], ln1_b[i])
        x = _layer_norm(DEEPNORM_ALPHA * x + _memory_cross_attention(x, mem, xa_wq[i], xa_wkv[i], xa_wo[i]),
                        lnx_g[i], lnx_b[i])
        if i % 2 == 0:
            ffn = _swiglu(x, ffn_w1[j], ffn_w3[j], ffn_w2[j])
        else:
            ffn = _moe_swiglu(x, moe_router_w[j], moe_router_b[j], moe_w1[j], moe_w3[j], moe_w2[j])
        x = _layer_norm(DEEPNORM_ALPHA * x + ffn, ln2_g[i], ln2_b[i])
    return x
```

```python
import functools
import math

import jax
import jax.numpy as jnp
from jax import lax
from jax.experimental import pallas as pl
from jax.experimental.pallas import tpu as pltpu

F32 = jnp.float32
BF16 = jnp.bfloat16
I32 = jnp.int32

LN_EPS = 1e-5
NEG_BIG = -1e30

CHUNK = 64
SB_HEADS = 4
HEAD_W = 128
DIFF_HEADS = 4
DIFF_HEAD_DIM = 64
ROPE_THETA = 10000.0
CONV_WIDTH = 31
CONV_HALO = 32
XA_HEADS = 4
N_EXPERTS = 8
TOP_K = 2

VMEM_LIMIT = 56 * 1024 * 1024
ROW_TILE = 512
ATT_TILE = 256
FF_CHUNK = 256
CONV_ROWS = 8
ROUTER_TILE = 128
SRC_TILE = 512
DISP_BLOCK = 256
EXPERT_BLOCK = 512
COMB_TILE = 128
COMB_ALIGN = 16
COMB_WIN = COMB_TILE + COMB_ALIGN


def _cparams(sem):
    return pltpu.CompilerParams(dimension_semantics=sem, vmem_limit_bytes=VMEM_LIMIT)


def _layer_norm(y, g, b):
    mu = jnp.mean(y, axis=-1, keepdims=True)
    d = y - mu
    var = jnp.mean(d * d, axis=-1, keepdims=True)
    return d * lax.rsqrt(var + LN_EPS) * g + b


def _sigmoid(x):
    return 1.0 / (1.0 + jnp.exp(-x))


def _dot(a, b):
    return jnp.dot(a, b, preferred_element_type=F32)


def _dot_nt(a, b):
    return lax.dot_general(a, b, (((1,), (1,)), ((), ())), preferred_element_type=F32)


def _const_spec(shape):
    nd = len(shape)
    return pl.BlockSpec(shape, lambda *_: (0,) * nd)


def _matmul_kernel(x_ref, w_ref, o_ref):
    o_ref[...] = _dot(x_ref[...].astype(BF16), w_ref[...]).astype(o_ref.dtype)


def _matmul(x, w, name):
    n, d = x.shape
    m = w.shape[1]
    tm = min(ROW_TILE, n)
    return pl.pallas_call(
        _matmul_kernel,
        grid=(n // tm,),
        in_specs=[pl.BlockSpec((tm, d), lambda i: (i, 0)), _const_spec((d, m))],
        out_specs=pl.BlockSpec((tm, m), lambda i: (i, 0)),
        out_shape=jax.ShapeDtypeStruct((n, m), BF16),
        compiler_params=_cparams(("parallel",)),
        name=name,
    )(x, w)


def _inproj_kernel(x_ref, w_ref, cos_ref, sa_ref, sb_ref, o_ref, *, width, q_scale):
    xb = x_ref[...].astype(BF16)
    n_groups = o_ref.shape[1] // width
    for j in range(n_groups):
        acc = _dot(xb, w_ref[:, j * width:(j + 1) * width])
        if j in (3, 4):
            parts = []
            for c in range(width // HEAD_W):
                hc = acc[:, c * HEAD_W:(c + 1) * HEAD_W]
                rot = (hc * cos_ref[...]
                       + pltpu.roll(hc, HEAD_W - DIFF_HEAD_DIM // 2, 1) * sa_ref[...]
                       + pltpu.roll(hc, DIFF_HEAD_DIM // 2, 1) * sb_ref[...])
                parts.append(rot)
            acc = jnp.concatenate(parts, axis=1)
            if j == 3:
                acc = acc * q_scale
        o_ref[:, j * width:(j + 1) * width] = acc.astype(o_ref.dtype)


def _inproj(x, w, cos_t, sa_t, sb_t, seq):
    n, d = x.shape
    m = w.shape[1]
    tm = min(ROW_TILE, seq)
    per_seq = seq // tm
    tab = pl.BlockSpec((tm, HEAD_W), lambda i: (i % per_seq, 0))
    return pl.pallas_call(
        functools.partial(_inproj_kernel, width=m // 6, q_scale=DIFF_HEAD_DIM ** -0.5),
        grid=(n // tm,),
        in_specs=[pl.BlockSpec((tm, d), lambda i: (i, 0)), _const_spec((d, m)), tab, tab, tab],
        out_specs=pl.BlockSpec((tm, m), lambda i: (i, 0)),
        out_shape=jax.ShapeDtypeStruct((n, m), BF16),
        compiler_params=_cparams(("parallel",)),
        name="attn_inproj",
    )(x, w, cos_t, sa_t, sb_t)


def _rope_tables(seq):
    half = DIFF_HEAD_DIM // 2
    inv = ROPE_THETA ** (-jnp.arange(0, DIFF_HEAD_DIM, 2, dtype=F32) / DIFF_HEAD_DIM)
    ang = jnp.arange(seq, dtype=F32)[:, None] * inv[None, :]
    cos, sin = jnp.cos(ang), jnp.sin(ang)
    zero = jnp.zeros_like(sin)
    reps = HEAD_W // DIFF_HEAD_DIM
    cos_t = jnp.tile(jnp.concatenate([cos, cos], axis=1), (1, reps))
    sa_t = jnp.tile(jnp.concatenate([-sin, zero], axis=1), (1, reps))
    sb_t = jnp.tile(jnp.concatenate([zero, sin], axis=1), (1, reps))
    del half
    return cos_t, sa_t, sb_t


def _sb_kernel(q_ref, k_ref, v_ref, o_ref, *, tile, scale):
    i = pl.program_id(2)
    q = q_ref[...]
    row = lax.broadcasted_iota(I32, (tile, tile), 0)
    col = lax.broadcasted_iota(I32, (tile, tile), 1)
    suffix = jnp.where(row >= col, 1.0, 0.0).astype(BF16)
    strict = col < row

    def block(kb, carry, acc, diagonal):
        k0 = pl.multiple_of(kb * tile, tile)
        z = _dot_nt(q, k_ref[pl.ds(k0, tile), :]) * scale
        log_keep = -(jnp.maximum(z, 0.0) + jnp.log(1.0 + jnp.exp(-jnp.abs(z))))
        if diagonal:
            log_keep = jnp.where(strict, log_keep, 0.0)
        hi = log_keep.astype(BF16)
        lo = (log_keep - hi.astype(F32)).astype(BF16)
        within = _dot(hi, suffix) + _dot(lo, suffix)
        w = jnp.exp(z + within + carry)
        if diagonal:
            w = jnp.where(strict, w, 0.0)
        acc = acc + _dot(w.astype(BF16), v_ref[pl.ds(k0, tile), :])
        carry = carry + jnp.sum(log_keep, axis=1, keepdims=True)
        return carry, acc

    carry = jnp.zeros((tile, 1), F32)
    acc = jnp.zeros((tile, HEAD_W), F32)
    carry, acc = block(i, carry, acc, True)
    carry, acc = lax.fori_loop(0, i, lambda t, c: block(i - 1 - t, c[0], c[1], False), (carry, acc))
    o_ref[...] = acc.astype(o_ref.dtype)


def _sb_attention(qkv, batch, seq):
    tile = min(ATT_TILE, seq)
    nq = seq // tile
    return pl.pallas_call(
        functools.partial(_sb_kernel, tile=tile, scale=HEAD_W ** -0.5),
        grid=(batch, SB_HEADS, nq),
        in_specs=[
            pl.BlockSpec((tile, HEAD_W), lambda b, h, i: (b * nq + i, h)),
            pl.BlockSpec((seq, HEAD_W), lambda b, h, i: (b, SB_HEADS + h)),
            pl.BlockSpec((seq, HEAD_W), lambda b, h, i: (b, 2 * SB_HEADS + h)),
        ],
        out_specs=pl.BlockSpec((tile, HEAD_W), lambda b, h, i: (b * nq + i, h)),
        out_shape=jax.ShapeDtypeStruct((batch * seq, SB_HEADS * HEAD_W), BF16),
        compiler_params=_cparams(("parallel", "parallel", "arbitrary")),
        name="stickbreak_attn",
    )(qkv, qkv, qkv)


def _diff_kernel(lq1_ref, lk1_ref, lq2_ref, lk2_ref, g_ref, q_ref, k_ref, v_ref, o_ref, *,
                 tile, lambda_init):
    i = pl.program_id(2)
    q = q_ref[...]
    lane = lax.broadcasted_iota(I32, (tile, HEAD_W), 1)
    zero = jnp.zeros_like(q)
    qq = jnp.concatenate([jnp.where(lane < DIFF_HEAD_DIM, q, zero),
                          jnp.where(lane >= DIFF_HEAD_DIM, q, zero)], axis=0)
    row = lax.broadcasted_iota(I32, (2 * tile, tile), 0)
    col = lax.broadcasted_iota(I32, (2 * tile, tile), 1)
    q_pos = jnp.where(row >= tile, row - tile, row)
    allowed = (col // CHUNK) <= (q_pos // CHUNK)

    def block(kb, m, l, acc, diagonal):
        k0 = pl.multiple_of(kb * tile, tile)
        z = _dot_nt(qq, k_ref[pl.ds(k0, tile), :])
        if diagonal:
            z = jnp.where(allowed, z, NEG_BIG)
        m_new = jnp.maximum(m, jnp.max(z, axis=1, keepdims=True))
        p = jnp.exp(z - m_new)
        a = jnp.exp(m - m_new)
        l = a * l + jnp.sum(p, axis=1, keepdims=True)
        acc = a * acc + _dot(p.astype(BF16), v_ref[pl.ds(k0, tile), :])
        return m_new, l, acc

    m = jnp.full((2 * tile, 1), NEG_BIG, F32)
    l = jnp.zeros((2 * tile, 1), F32)
    acc = jnp.zeros((2 * tile, HEAD_W), F32)
    m, l, acc = lax.fori_loop(0, i, lambda kb, c: block(kb, c[0], c[1], c[2], False), (m, l, acc))
    m, l, acc = block(i, m, l, acc, True)

    lam = (jnp.exp(jnp.sum(lq1_ref[...] * lk1_ref[...], axis=1, keepdims=True))
           - jnp.exp(jnp.sum(lq2_ref[...] * lk2_ref[...], axis=1, keepdims=True)) + lambda_init)
    o = acc[:tile] / l[:tile] - lam * (acc[tile:] / l[tile:])
    rms = lax.rsqrt(jnp.mean(o * o, axis=-1, keepdims=True) + LN_EPS)
    o_ref[...] = (o * rms * g_ref[...] * (1.0 - lambda_init)).astype(o_ref.dtype)


def _diff_attention(qkv, lq1, lk1, lq2, lk2, subln_g, batch, seq, lambda_init):
    tile = min(ATT_TILE, seq)
    nq = seq // tile
    base = 3 * SB_HEADS
    vec = _const_spec((1, DIFF_HEAD_DIM))
    return pl.pallas_call(
        functools.partial(_diff_kernel, tile=tile, lambda_init=lambda_init),
        grid=(batch, DIFF_HEADS, nq),
        in_specs=[
            vec, vec, vec, vec, _const_spec((1, HEAD_W)),
            pl.BlockSpec((tile, HEAD_W), lambda b, h, i: (b * nq + i, base + h)),
            pl.BlockSpec((seq, HEAD_W), lambda b, h, i: (b, base + DIFF_HEADS + h)),
            pl.BlockSpec((seq, HEAD_W), lambda b, h, i: (b, base + 2 * DIFF_HEADS + h)),
        ],
        out_specs=pl.BlockSpec((tile, HEAD_W), lambda b, h, i: (b * nq + i, h)),
        out_shape=jax.ShapeDtypeStruct((batch * seq, DIFF_HEADS * HEAD_W), BF16),
        compiler_params=_cparams(("parallel", "parallel", "arbitrary")),
        name="diff_attn",
    )(lq1[None, :], lk1[None, :], lq2[None, :], lk2[None, :], subln_g[None, :], qkv, qkv, qkv)


def _proj2_ln_kernel(a_ref, b_ref, wa_ref, wb_ref, x_ref, g_ref, beta_ref, o_ref, *, alpha):
    y = _dot(a_ref[...], wa_ref[...]) + _dot(b_ref[...], wb_ref[...])
    o_ref[...] = _layer_norm(alpha * x_ref[...] + y, g_ref[...], beta_ref[...])


def _proj2_ln(a, b, wa, wb, x, g, beta, alpha):
    n, d = x.shape
    tm = min(ROW_TILE, n)
    ka, kb = a.shape[1], b.shape[1]
    row = lambda w: pl.BlockSpec((tm, w), lambda i: (i, 0))
    return pl.pallas_call(
        functools.partial(_proj2_ln_kernel, alpha=alpha),
        grid=(n // tm,),
        in_specs=[row(ka), row(kb), _const_spec((ka, d)), _const_spec((kb, d)), row(d),
                  _const_spec((1, d)), _const_spec((1, d))],
        out_specs=row(d),
        out_shape=jax.ShapeDtypeStruct((n, d), F32),
        compiler_params=_cparams(("parallel",)),
        name="attn_outproj_ln",
    )(a, b, wa, wb, x, g[None, :], beta[None, :])


def _xattn_kernel(x_ref, kv_ref, wq_ref, wo_ref, g_ref, beta_ref, o_ref, *, alpha, scale):
    x = x_ref[...]
    q = _dot(x.astype(BF16), wq_ref[...]).astype(BF16)
    width = wq_ref.shape[1]
    outs = []
    for h in range(XA_HEADS):
        qh = q[:, h * HEAD_W:(h + 1) * HEAD_W]
        kh = kv_ref[:, h * HEAD_W:(h + 1) * HEAD_W]
        vh = kv_ref[:, width + h * HEAD_W:width + (h + 1) * HEAD_W]
        z = _dot_nt(qh, kh) * scale
        z = z - jnp.max(z, axis=1, keepdims=True)
        p = jnp.exp(z)
        p = p / jnp.sum(p, axis=1, keepdims=True)
        outs.append(_dot(p.astype(BF16), vh))
    o = jnp.concatenate(outs, axis=1).astype(BF16)
    y = _dot(o, wo_ref[...])
    o_ref[...] = _layer_norm(alpha * x + y, g_ref[...], beta_ref[...])


def _xattn_ln(x, kv, wq, wo, g, beta, seq, n_mem, alpha):
    n, d = x.shape
    tm = min(ROW_TILE, seq)
    per_seq = seq // tm
    width = wq.shape[1]
    return pl.pallas_call(
        functools.partial(_xattn_kernel, alpha=alpha, scale=HEAD_W ** -0.5),
        grid=(n // tm,),
        in_specs=[pl.BlockSpec((tm, d), lambda i: (i, 0)),
                  pl.BlockSpec((n_mem, 2 * width), lambda i: (i // per_seq, 0)),
                  _const_spec((d, width)), _const_spec((width, d)),
                  _const_spec((1, d)), _const_spec((1, d))],
        out_specs=pl.BlockSpec((tm, d), lambda i: (i, 0)),
        out_shape=jax.ShapeDtypeStruct((n, d), F32),
        compiler_params=_cparams(("parallel",)),
        name="xattn_ln",
    )(x, kv, wq, wo, g[None, :], beta[None, :])


def _swiglu_acc(xb, w1_ref, w3_ref, w2_ref, rows, d_out):
    d_ff = w1_ref.shape[1]
    acc = jnp.zeros((rows, d_out), F32)
    for c in range(d_ff // FF_CHUNK):
        sl = slice(c * FF_CHUNK, (c + 1) * FF_CHUNK)
        a = _dot(xb, w1_ref[:, sl])
        b = _dot(xb, w3_ref[:, sl])
        h = (a * _sigmoid(a) * b).astype(BF16)
        acc = acc + _dot(h, w2_ref[sl, :])
    return acc


def _ffn_kernel(x_ref, w1_ref, w3_ref, w2_ref, g_ref, beta_ref, o_ref, *, alpha):
    x = x_ref[...]
    y = _swiglu_acc(x.astype(BF16), w1_ref, w3_ref, w2_ref, x.shape[0], x.shape[1])
    o_ref[...] = _layer_norm(alpha * x + y, g_ref[...], beta_ref[...])


def _ffn_ln(x, w1, w3, w2, g, beta, alpha):
    n, d = x.shape
    f = w1.shape[1]
    assert f % FF_CHUNK == 0
    tm = min(ROW_TILE, n)
    return pl.pallas_call(
        functools.partial(_ffn_kernel, alpha=alpha),
        grid=(n // tm,),
        in_specs=[pl.BlockSpec((tm, d), lambda i: (i, 0)),
                  _const_spec((d, f)), _const_spec((d, f)), _const_spec((f, d)),
                  _const_spec((1, d)), _const_spec((1, d))],
        out_specs=pl.BlockSpec((tm, d), lambda i: (i, 0)),
        out_shape=jax.ShapeDtypeStruct((n, d), F32),
        compiler_params=_cparams(("parallel",)),
        name="ffn_ln",
    )(x, w1, w3, w2, g[None, :], beta[None, :])


def _conv_kernel(x_ref, xp_ref, wpw1_ref, bpw1_ref, dww_ref, dwb_ref, cg_ref, cb_ref,
                 wpw2_ref, bpw2_ref, g_ref, beta_ref, o_ref, h_scr, c_scr, *, alpha, per_seq):
    i = pl.program_id(0)
    tm, ch = x_ref.shape

    def glu(xv):
        xb = xv.astype(BF16)
        a = _dot(xb, wpw1_ref[:, :ch]) + bpw1_ref[:, :ch]
        g = _dot(xb, wpw1_ref[:, ch:]) + bpw1_ref[:, ch:]
        return a * _sigmoid(g)

    halo = glu(xp_ref[...])
    h_scr[0:CONV_HALO, :] = jnp.where(i % per_seq == 0, 0.0, halo)
    x = x_ref[...]
    h_scr[CONV_HALO:CONV_HALO + tm, :] = glu(x)
    h_scr[CONV_HALO + tm:CONV_HALO + tm + CONV_ROWS, :] = jnp.zeros((CONV_ROWS, ch), F32)

    first = CONV_HALO - (CONV_WIDTH - 1)
    sub = lax.broadcasted_iota(I32, (CONV_ROWS, HEAD_W), 0)

    for lt in range(ch // HEAD_W):
        lanes = slice(lt * HEAD_W, (lt + 1) * HEAD_W)

        def partials(base):
            out = []
            for s in range(CONV_ROWS):
                acc = None
                for a in range((first + CONV_WIDTH - 1) // CONV_ROWS + 1):
                    k = CONV_ROWS * a + s - first
                    if 0 <= k < CONV_WIDTH:
                        r = pl.multiple_of(base + CONV_ROWS * a, CONV_ROWS)
                        term = h_scr[pl.ds(r, CONV_ROWS), lanes] * dww_ref[k:k + 1, lanes]
                        acc = term if acc is None else acc + term
                out.append(acc)
            return tuple(out)

        def rows(c, g_old):
            r0 = pl.multiple_of(c * CONV_ROWS, CONV_ROWS)
            g_new = partials(r0 + CONV_ROWS)
            acc = g_old[0]
            for s in range(1, CONV_ROWS):
                acc = acc + pltpu.roll(jnp.where(sub >= s, g_old[s], g_new[s]), CONV_ROWS - s, 0)
            c_scr[pl.ds(r0, CONV_ROWS), lanes] = acc
            return g_new

        lax.fori_loop(0, tm // CONV_ROWS, rows, partials(0))

    y = _layer_norm(c_scr[...] + dwb_ref[...], cg_ref[...], cb_ref[...])
    y = _dot((y * _sigmoid(y)).astype(BF16), wpw2_ref[...]) + bpw2_ref[...]
    o_ref[...] = _layer_norm(alpha * x + y, g_ref[...], beta_ref[...])


def _conv_ln(x, wpw1, bpw1, dww, dwb, cg, cb, wpw2, bpw2, g, beta, seq, alpha):
    n, d = x.shape
    tm = min(ROW_TILE, seq)
    per_seq = seq // tm
    halo_per_tile = tm // CONV_HALO
    vec = lambda w: _const_spec((1, w))
    return pl.pallas_call(
        functools.partial(_conv_kernel, alpha=alpha, per_seq=per_seq),
        grid=(n // tm,),
        in_specs=[pl.BlockSpec((tm, d), lambda i: (i, 0)),
                  pl.BlockSpec((CONV_HALO, d), lambda i: (jnp.maximum(i * halo_per_tile - 1, 0), 0)),
                  _const_spec((d, 2 * d)), vec(2 * d), _const_spec((CONV_WIDTH, d)), vec(d), vec(d), vec(d),
                  _const_spec((d, d)), vec(d), vec(d), vec(d)],
        out_specs=pl.BlockSpec((tm, d), lambda i: (i, 0)),
        out_shape=jax.ShapeDtypeStruct((n, d), F32),
        scratch_shapes=[pltpu.VMEM((tm + CONV_HALO + CONV_ROWS, d), F32), pltpu.VMEM((tm, d), F32)],
        compiler_params=_cparams(("parallel",)),
        name="conv_module_ln",
    )(x, x, wpw1, bpw1[None, :], dww, dwb[None, :], cg[None, :], cb[None, :],
      wpw2, bpw2[None, :], g[None, :], beta[None, :])


def _router_kernel(x_ref, w_ref, b_ref, meta_ref, before_ref, total_ref, count_scr):
    i = pl.program_id(0)
    tm = x_ref.shape[0]

    @pl.when(i == 0)
    def _():
        count_scr[...] = jnp.zeros_like(count_scr)

    logits = jnp.dot(x_ref[...], w_ref[...], preferred_element_type=F32,
                     precision=lax.Precision.HIGHEST) + b_ref[...]
    lane = lax.broadcasted_iota(I32, logits.shape, 1)
    m1 = jnp.max(logits, axis=1, keepdims=True)
    i1 = jnp.min(jnp.where(logits == m1, lane, HEAD_W), axis=1, keepdims=True)
    rest = jnp.where(lane == i1, NEG_BIG * 2, logits)
    m2 = jnp.max(rest, axis=1, keepdims=True)
    i2 = jnp.min(jnp.where(rest == m2, lane, HEAD_W), axis=1, keepdims=True)
    t = jnp.exp(m2 - m1)
    g1 = 1.0 / (1.0 + t)
    g2 = t / (1.0 + t)

    pick1 = lane == i1
    pick2 = lane == i2
    chosen = jnp.where(pick1, 1.0, jnp.where(pick2, 1.0, 0.0))
    r = lax.broadcasted_iota(I32, (tm, tm), 0)
    c = lax.broadcasted_iota(I32, (tm, tm), 1)
    earlier = jnp.where(c < r, 1.0, 0.0).astype(BF16)
    before = count_scr[0:1, :]
    rank = _dot(earlier, chosen.astype(BF16)) + before
    r1 = jnp.sum(jnp.where(pick1, rank, 0.0), axis=1, keepdims=True)
    r2 = jnp.sum(jnp.where(pick2, rank, 0.0), axis=1, keepdims=True)

    meta = jnp.where(lane == 0, i1.astype(F32), 0.0)
    meta = jnp.where(lane == 1, i2.astype(F32), meta)
    meta = jnp.where(lane == 2, r1, meta)
    meta = jnp.where(lane == 3, r2, meta)
    meta = jnp.where(lane == 4, g1, meta)
    meta = jnp.where(lane == 5, g2, meta)
    meta_ref[...] = meta

    before_ref[...] = jnp.broadcast_to(before, before_ref.shape)
    total = before + jnp.sum(chosen, axis=0, keepdims=True)
    count_scr[...] = jnp.broadcast_to(total, count_scr.shape)
    total_ref[...] = jnp.broadcast_to(total, total_ref.shape)


def _router(x, w_pad, b_pad):
    n, d = x.shape
    tm = min(ROUTER_TILE, n)
    nt = n // tm
    return pl.pallas_call(
        _router_kernel,
        grid=(nt,),
        in_specs=[pl.BlockSpec((tm, d), lambda i: (i, 0)), _const_spec((d, HEAD_W)), _const_spec((1, HEAD_W))],
        out_specs=[pl.BlockSpec((tm, HEAD_W), lambda i: (i, 0)),
                   pl.BlockSpec((8, HEAD_W), lambda i: (i, 0)),
                   _const_spec((8, HEAD_W))],
        out_shape=[jax.ShapeDtypeStruct((n, HEAD_W), F32),
                   jax.ShapeDtypeStruct((nt * 8, HEAD_W), F32),
                   jax.ShapeDtypeStruct((8, HEAD_W), F32)],
        scratch_shapes=[pltpu.VMEM((8, HEAD_W), F32)],
        compiler_params=_cparams(("arbitrary",)),
        name="moe_router",
    )(x, w_pad, b_pad)


def _dispatch_kernel(vb_ref, vt_ref, vf_ref, vv_ref, x_ref, d1_ref, d2_ref, o_ref):
    v = pl.program_id(0)
    rows, ts = o_ref.shape[0], x_ref.shape[0]

    @pl.when(vf_ref[v] == 1)
    def _():
        o_ref[...] = jnp.zeros_like(o_ref)

    @pl.when(vv_ref[v] == 1)
    def _():
        rid = vb_ref[v] * rows + lax.broadcasted_iota(I32, (rows, ts), 0)
        onehot = jnp.where(d1_ref[...] == rid, 1.0, jnp.where(d2_ref[...] == rid, 1.0, 0.0)).astype(BF16)
        o_ref[...] += _dot(onehot, x_ref[...].astype(BF16)).astype(o_ref.dtype)


def _dispatch(x, dest1, dest2, vis, n_rows):
    n, d = x.shape
    ts = min(SRC_TILE, n)
    n_src = n // ts
    n_vis = vis[0].shape[0]
    dspec = pl.BlockSpec((None, 1, ts), lambda v, vb, vt, vf, vv: (vt[v], 0, 0))
    grid_spec = pltpu.PrefetchScalarGridSpec(
        num_scalar_prefetch=4,
        grid=(n_vis,),
        in_specs=[pl.BlockSpec((ts, d), lambda v, vb, vt, vf, vv: (vt[v], 0)), dspec, dspec],
        out_specs=pl.BlockSpec((DISP_BLOCK, d), lambda v, vb, vt, vf, vv: (vb[v], 0)),
    )
    return pl.pallas_call(
        _dispatch_kernel,
        grid_spec=grid_spec,
        out_shape=jax.ShapeDtypeStruct((n_rows, d), BF16),
        compiler_params=_cparams(("arbitrary",)),
        name="moe_dispatch",
    )(*vis, x, dest1.reshape(n_src, 1, ts), dest2.reshape(n_src, 1, ts))


def _expert_kernel(be_ref, bv_ref, x_ref, w1_ref, w3_ref, w2_ref, o_ref):
    j = pl.program_id(0)

    @pl.when(bv_ref[j] == 1)
    def _():
        y = _swiglu_acc(x_ref[...], w1_ref, w3_ref, w2_ref, x_ref.shape[0], o_ref.shape[1])
        o_ref[...] = y.astype(o_ref.dtype)

    @pl.when(bv_ref[j] == 0)
    def _():
        o_ref[...] = jnp.zeros_like(o_ref)


def _experts(xs, w1, w3, w2, block_e, block_valid):
    n_rows, d = xs.shape
    f = w1.shape[2]
    assert f % FF_CHUNK == 0
    n_blocks = n_rows // EXPERT_BLOCK
    n_steps = block_e.shape[0]
    grid_spec = pltpu.PrefetchScalarGridSpec(
        num_scalar_prefetch=2,
        grid=(n_steps,),
        in_specs=[pl.BlockSpec((EXPERT_BLOCK, d), lambda j, be, bv: (jnp.minimum(j, n_blocks - 1), 0)),
                  pl.BlockSpec((None, d, f), lambda j, be, bv: (be[j], 0, 0)),
                  pl.BlockSpec((None, d, f), lambda j, be, bv: (be[j], 0, 0)),
                  pl.BlockSpec((None, f, d), lambda j, be, bv: (be[j], 0, 0))],
        out_specs=pl.BlockSpec((EXPERT_BLOCK, d), lambda j, be, bv: (j, 0)),
    )
    return pl.pallas_call(
        _expert_kernel,
        grid_spec=grid_spec,
        out_shape=jax.ShapeDtypeStruct((n_steps * EXPERT_BLOCK, d), BF16),
        compiler_params=_cparams(("arbitrary",)),
        name="moe_experts",
    )(block_e, block_valid, xs, w1, w3, w2)


def _combine_kernel(wb_ref, x_ref, cm_ref, y_hbm, g_ref, beta_ref, o_ref, win, sem, *, alpha):
    i = pl.program_id(0)
    n_steps = pl.num_programs(0)
    tm = x_ref.shape[0]

    def window_copy(step, slot, e):
        start = pl.multiple_of(wb_ref[step * N_EXPERTS + e], COMB_ALIGN)
        return pltpu.make_async_copy(y_hbm.at[pl.ds(start, COMB_WIN), :],
                                     win.at[slot, pl.ds(e * COMB_WIN, COMB_WIN), :],
                                     sem.at[slot, e])

    @pl.when(i == 0)
    def _():
        for e in range(N_EXPERTS):
            window_copy(0, 0, e).start()

    slot = i % 2

    @pl.when(i + 1 < n_steps)
    def _():
        for e in range(N_EXPERTS):
            window_copy(i + 1, 1 - slot, e).start()

    for e in range(N_EXPERTS):
        window_copy(i, slot, e).wait()

    cm = cm_ref[...]
    col = lax.broadcasted_iota(I32, (tm, N_EXPERTS * COMB_WIN), 1).astype(F32)
    p1 = jnp.where(col == cm[:, 0:1], 1.0, 0.0).astype(BF16)
    p2 = jnp.where(col == cm[:, 1:2], 1.0, 0.0).astype(BF16)
    rows = _dot(jnp.concatenate([p1, p2], axis=0), win[slot])
    y = cm[:, 2:3] * rows[:tm] + cm[:, 3:4] * rows[tm:]
    o_ref[...] = _layer_norm(alpha * x_ref[...] + y, g_ref[...], beta_ref[...])


def _combine_ln(x, cm, wbase, yb, g, beta, alpha):
    n, d = x.shape
    tm = min(COMB_TILE, n)
    grid_spec = pltpu.PrefetchScalarGridSpec(
        num_scalar_prefetch=1,
        grid=(n // tm,),
        in_specs=[pl.BlockSpec((tm, d), lambda i, wb: (i, 0)),
                  pl.BlockSpec((tm, 8), lambda i, wb: (i, 0)),
                  pl.BlockSpec(memory_space=pl.ANY),
                  pl.BlockSpec((1, d), lambda i, wb: (0, 0)),
                  pl.BlockSpec((1, d), lambda i, wb: (0, 0))],
        out_specs=pl.BlockSpec((tm, d), lambda i, wb: (i, 0)),
        scratch_shapes=[pltpu.VMEM((2, N_EXPERTS * COMB_WIN, d), BF16),
                        pltpu.SemaphoreType.DMA((2, N_EXPERTS))],
    )
    return pl.pallas_call(
        functools.partial(_combine_kernel, alpha=alpha),
        grid_spec=grid_spec,
        out_shape=jax.ShapeDtypeStruct((n, d), F32),
        compiler_params=_cparams(("arbitrary",)),
        name="moe_combine_ln",
    )(wbase, x, cm, yb, g[None, :], beta[None, :])


def _moe_ln(x, router_w, router_b, w1, w3, w2, g, beta, alpha):
    n, d = x.shape
    n_exp = router_w.shape[1]
    assert n_exp == N_EXPERTS
    w_pad = jnp.zeros((d, HEAD_W), F32).at[:, :n_exp].set(router_w)
    b_pad = jnp.full((1, HEAD_W), NEG_BIG, F32).at[0, :n_exp].set(router_b)
    meta, before, total = _router(x, w_pad, b_pad)

    e1, e2 = meta[:, 0].astype(I32), meta[:, 1].astype(I32)
    r1, r2 = meta[:, 2].astype(I32), meta[:, 3].astype(I32)
    counts = total[0, :n_exp].astype(I32)
    padded = (counts + EXPERT_BLOCK - 1) // EXPERT_BLOCK * EXPERT_BLOCK
    pend = jnp.cumsum(padded)
    pstart = pend - padded
    dest1 = pstart[e1] + r1
    dest2 = pstart[e2] + r2
    n_blocks = (n * TOP_K) // EXPERT_BLOCK + n_exp
    n_rows = n_blocks * EXPERT_BLOCK

    def owner(starts):
        return jnp.minimum(jnp.sum((starts[:, None] >= pend[None, :]).astype(I32), axis=1), n_exp - 1)

    bstart = jnp.arange(n_blocks + 1, dtype=I32) * EXPERT_BLOCK
    block_e = owner(bstart)
    block_valid = (bstart < (pstart + counts)[block_e]).astype(I32)

    rt = min(ROUTER_TILE, n)
    before_rt = before[::8, :n_exp].astype(I32)
    ts = min(SRC_TILE, n)
    before_src = before_rt[:: ts // rt]
    n_src = n // ts

    n_db = n_rows // DISP_BLOCK
    dstart = jnp.arange(n_db, dtype=I32) * DISP_BLOCK
    d_e = owner(dstart)
    rs = dstart - pstart[d_e]
    re = jnp.minimum(rs + DISP_BLOCK, counts[d_e])
    d_valid = rs < counts[d_e]
    cnt_before = before_src[:, d_e].T
    t_first = jnp.sum((cnt_before <= rs[:, None]).astype(I32), axis=1) - 1
    t_last = jnp.sum((cnt_before <= (re - 1)[:, None]).astype(I32), axis=1) - 1
    t_first = jnp.where(d_valid, t_first, 0)
    t_last = jnp.where(d_valid, t_last, 0)
    n_vis = t_last - t_first + 1
    v_start = jnp.cumsum(n_vis) - n_vis
    v_total = jnp.sum(n_vis)
    max_vis = n_db + n_exp * n_src
    v = jnp.arange(max_vis, dtype=I32)
    vb = jnp.clip(jnp.searchsorted(v_start, v, side="right").astype(I32) - 1, 0, n_db - 1)
    vt = jnp.clip(t_first[vb] + v - v_start[vb], 0, n_src - 1)
    vvalid = (v < v_total).astype(I32)
    vfirst = (v == v_start[vb]).astype(I32)
    xs = _dispatch(x, dest1, dest2, (vb, vt, vfirst, vvalid), n_rows)

    yb = _experts(xs, w1, w3, w2, block_e, block_valid)

    ct = min(COMB_TILE, n)
    before_ct = before_rt[:: ct // rt]
    wbase = (pstart[None, :] + before_ct) // COMB_ALIGN * COMB_ALIGN
    tile_of = jnp.arange(n, dtype=I32) // ct
    c1 = e1 * COMB_WIN + dest1 - wbase[tile_of, e1]
    c2 = e2 * COMB_WIN + dest2 - wbase[tile_of, e2]
    cm = jnp.stack([c1.astype(F32), c2.astype(F32), meta[:, 4], meta[:, 5]]
                   + [jnp.zeros((n,), F32)] * 4, axis=1)
    return _combine_ln(x, cm, wbase.reshape(-1).astype(I32), yb, g, beta, alpha)


def kernel(x, mem, ln1_g, ln1_b, lnx_g, lnx_b, ln2_g, ln2_b, xa_wq, xa_wkv, xa_wo, attn_w_in, attn_w_o, diff_lq1, diff_lk1, diff_lq2, diff_lk2, diff_subln_g, ffn_w1, ffn_w3, ffn_w2, conv_w_pw1, conv_b_pw1, conv_dw_w, conv_dw_b, conv_ln_g, conv_ln_b, conv_w_pw2, conv_b_pw2, moe_router_w, moe_router_b, moe_w1, moe_w3, moe_w2):
    batch, seq, d = x.shape
    n_mem = mem.shape[1]
    depth = ln1_g.shape[0]
    alpha = (2 * depth) ** 0.25
    sb_width = SB_HEADS * HEAD_W

    bf = lambda w: w.astype(BF16)
    cos_t, sa_t, sb_t = _rope_tables(seq)
    h = x.reshape(batch * seq, d)
    mem2 = mem.reshape(batch * n_mem, d)

    for i in range(depth):
        j = i // 2
        if i % 2 == 0:
            lambda_init = 0.8 - 0.6 * math.exp(-0.3 * i)
            qkv = _inproj(h, bf(attn_w_in[j]), cos_t, sa_t, sb_t, seq)
            o_sb = _sb_attention(qkv, batch, seq)
            o_df = _diff_attention(qkv, diff_lq1[j], diff_lk1[j], diff_lq2[j], diff_lk2[j],
                                   diff_subln_g[j], batch, seq, lambda_init)
            w_o = bf(attn_w_o[j])
            h = _proj2_ln(o_sb, o_df, w_o[:sb_width], w_o[sb_width:], h, ln1_g[i], ln1_b[i], alpha)
        else:
            h = _conv_ln(h, bf(conv_w_pw1[j]), conv_b_pw1[j], conv_dw_w[j], conv_dw_b[j],
                         conv_ln_g[j], conv_ln_b[j], bf(conv_w_pw2[j]), conv_b_pw2[j],
                         ln1_g[i], ln1_b[i], seq, alpha)
        kv = _matmul(mem2, bf(xa_wkv[i]), "xattn_kv")
        h = _xattn_ln(h, kv, bf(xa_wq[i]), bf(xa_wo[i]), lnx_g[i], lnx_b[i], seq, n_mem, alpha)
        if i % 2 == 0:
            h = _ffn_ln(h, bf(ffn_w1[j]), bf(ffn_w3[j]), bf(ffn_w2[j]), ln2_g[i], ln2_b[i], alpha)
        else:
            h = _moe_ln(h, moe_router_w[j], moe_router_b[j], bf(moe_w1[j]), bf(moe_w3[j]), bf(moe_w2[j]),
                        ln2_g[i], ln2_b[i], alpha)
    return h.reshape(batch, seq, d)
```

```python
import functools
import math

import jax
import jax.numpy as jnp
from jax import lax
from jax.experimental import pallas as pl
from jax.experimental.pallas import tpu as pltpu

F32 = jnp.float32
BF16 = jnp.bfloat16
I32 = jnp.int32

LN_EPS = 1e-5
NEG_BIG = -1e30

CHUNK = 64
SB_HEADS = 4
HEAD_W = 128
DIFF_HEADS = 4
DIFF_HEAD_DIM = 64
ROPE_THETA = 10000.0
CONV_WIDTH = 31
CONV_HALO = 32
XA_HEADS = 4
N_EXPERTS = 8
TOP_K = 2

VMEM_LIMIT = 56 * 1024 * 1024
ROW_TILE = 512
SB_TQ = 1024
SB_TK = 256
SB_UNROLL = 4
DIFF_TQ = 512
DIFF_TK = 512
LOG2E = 1.4426950408889634
FF_CHUNK = 256
CONV_ROWS = 8
ROUTER_TILE = 128
SRC_TILE = 512
DISP_BLOCK = 256
EXPERT_BLOCK = 512
COMB_TILE = 128
COMB_ALIGN = 16
COMB_WIN = COMB_TILE + COMB_ALIGN


def _cparams(sem):
    return pltpu.CompilerParams(dimension_semantics=sem, vmem_limit_bytes=VMEM_LIMIT)


def _layer_norm(y, g, b):
    mu = jnp.mean(y, axis=-1, keepdims=True)
    d = y - mu
    var = jnp.mean(d * d, axis=-1, keepdims=True)
    return d * lax.rsqrt(var + LN_EPS) * g + b


def _sigmoid(x):
    return 1.0 / (1.0 + jnp.exp(-x))


def _dot(a, b):
    return jnp.dot(a, b, preferred_element_type=F32)


def _dot_nt(a, b):
    return lax.dot_general(a, b, (((1,), (1,)), ((), ())), preferred_element_type=F32)


def _const_spec(shape):
    nd = len(shape)
    return pl.BlockSpec(shape, lambda *_: (0,) * nd)


def _matmul_kernel(x_ref, w_ref, o_ref):
    o_ref[...] = _dot(x_ref[...].astype(BF16), w_ref[...]).astype(o_ref.dtype)


def _matmul(x, w, name):
    n, d = x.shape
    m = w.shape[1]
    tm = min(ROW_TILE, n)
    return pl.pallas_call(
        _matmul_kernel,
        grid=(n // tm,),
        in_specs=[pl.BlockSpec((tm, d), lambda i: (i, 0)), _const_spec((d, m))],
        out_specs=pl.BlockSpec((tm, m), lambda i: (i, 0)),
        out_shape=jax.ShapeDtypeStruct((n, m), BF16),
        compiler_params=_cparams(("parallel",)),
        name=name,
    )(x, w)


def _inproj_kernel(x_ref, w_ref, cos_ref, sa_ref, sb_ref, o_ref, *, width, sq_scale, dq_scale):
    xb = x_ref[...].astype(BF16)
    n_groups = o_ref.shape[1] // width
    for j in range(n_groups):
        acc = _dot(xb, w_ref[:, j * width:(j + 1) * width])
        if j in (3, 4):
            parts = []
            for c in range(width // HEAD_W):
                hc = acc[:, c * HEAD_W:(c + 1) * HEAD_W]
                rot = (hc * cos_ref[...]
                       + pltpu.roll(hc, HEAD_W - DIFF_HEAD_DIM // 2, 1) * sa_ref[...]
                       + pltpu.roll(hc, DIFF_HEAD_DIM // 2, 1) * sb_ref[...])
                parts.append(rot)
            acc = jnp.concatenate(parts, axis=1)
            if j == 3:
                acc = acc * dq_scale
        if j == 0:
            acc = acc * sq_scale
        o_ref[:, j * width:(j + 1) * width] = acc.astype(o_ref.dtype)


def _inproj(x, w, cos_t, sa_t, sb_t, seq):
    n, d = x.shape
    m = w.shape[1]
    tm = min(ROW_TILE, seq)
    per_seq = seq // tm
    tab = pl.BlockSpec((tm, HEAD_W), lambda i: (i % per_seq, 0))
    return pl.pallas_call(
        functools.partial(_inproj_kernel, width=m // 6, sq_scale=HEAD_W ** -0.5 * LOG2E,
                          dq_scale=DIFF_HEAD_DIM ** -0.5 * LOG2E),
        grid=(n // tm,),
        in_specs=[pl.BlockSpec((tm, d), lambda i: (i, 0)), _const_spec((d, m)), tab, tab, tab],
        out_specs=pl.BlockSpec((tm, m), lambda i: (i, 0)),
        out_shape=jax.ShapeDtypeStruct((n, m), BF16),
        compiler_params=_cparams(("parallel",)),
        name="attn_inproj",
    )(x, w, cos_t, sa_t, sb_t)


def _rope_tables(seq):
    half = DIFF_HEAD_DIM // 2
    inv = ROPE_THETA ** (-jnp.arange(0, DIFF_HEAD_DIM, 2, dtype=F32) / DIFF_HEAD_DIM)
    ang = jnp.arange(seq, dtype=F32)[:, None] * inv[None, :]
    cos, sin = jnp.cos(ang), jnp.sin(ang)
    zero = jnp.zeros_like(sin)
    reps = HEAD_W // DIFF_HEAD_DIM
    cos_t = jnp.tile(jnp.concatenate([cos, cos], axis=1), (1, reps))
    sa_t = jnp.tile(jnp.concatenate([-sin, zero], axis=1), (1, reps))
    sb_t = jnp.tile(jnp.concatenate([zero, sin], axis=1), (1, reps))
    del half
    return cos_t, sa_t, sb_t


def _sb_kernel(q_ref, k_ref, v_ref, o_ref, acc_ref, carry_ref, za_ref, zb_ref, *, tq, tk):
    i = pl.program_id(2)
    band = tq // tk
    row = lax.broadcasted_iota(I32, (tk, tk), 0)
    col = lax.broadcasted_iota(I32, (tk, tk), 1)
    suffix = jnp.where(row >= col, 1.0, 0.0).astype(BF16)

    def logits(kb):
        k0 = pl.multiple_of(kb * tk, tk)
        return _dot_nt(q_ref[...], k_ref[pl.ds(k0, tk), :])

    def rest(z, kb, rows, strict=None):
        k0 = pl.multiple_of(kb * tk, tk)
        drop = jnp.maximum(z, 0.0) + jnp.log(1.0 + jnp.exp2(-jnp.abs(z))) * LOG2E
        if strict is not None:
            drop = jnp.where(strict, drop, 0.0)
        hi = drop.astype(BF16)
        lo = (drop - hi.astype(F32)).astype(BF16)
        within = _dot(hi, suffix) + _dot(lo, suffix)
        w = jnp.exp2(z - within - carry_ref[rows, :])
        if strict is not None:
            w = jnp.where(strict, w, 0.0)
        acc_ref[rows, :] += _dot(w.astype(BF16), v_ref[pl.ds(k0, tk), :])
        carry_ref[rows, :] += within[:, 0:1]

    acc_ref[...] = jnp.zeros_like(acc_ref)
    carry_ref[...] = jnp.zeros_like(carry_ref)
    for c in reversed(range(band)):
        kb = i * band + c
        k0 = pl.multiple_of(kb * tk, tk)
        rows = slice(c * tk, tq)
        n = tq - c * tk
        strict = lax.broadcasted_iota(I32, (n, tk), 1) < lax.broadcasted_iota(I32, (n, tk), 0)
        rest(_dot_nt(q_ref[rows, :], k_ref[pl.ds(k0, tk), :]), kb, rows, strict)
    every = slice(0, tq)
    below = i * band

    zs = (za_ref, zb_ref)
    rem = below % SB_UNROLL

    def single(t, c):
        kb = below - 1 - t
        rest(logits(kb), kb, every)
        return c

    lax.fori_loop(0, rem, single, 0)
    top = below - rem
    za_ref[...] = logits(jnp.maximum(top - 1, 0))

    def body(t, c):
        kb = top - 1 - SB_UNROLL * t
        for u in range(SB_UNROLL):
            zs[(u + 1) % 2][...] = logits(jnp.maximum(kb - u - 1, 0))
            rest(zs[u % 2][...], kb - u, every)
        return c

    lax.fori_loop(0, top // SB_UNROLL, body, 0)
    o_ref[...] = acc_ref[...].astype(o_ref.dtype)


def _whole_seq_spec(seq, first):
    return pl.BlockSpec((seq, HEAD_W), lambda b, h, i: (b, first + h), pipeline_mode=pl.Buffered(1))


def _sb_attention(qkv, batch, seq):
    tq = min(SB_TQ, seq)
    tk = min(SB_TK, seq)
    assert SB_UNROLL % 2 == 0 and tq % tk == 0
    nq = seq // tq
    return pl.pallas_call(
        functools.partial(_sb_kernel, tq=tq, tk=tk),
        grid=(batch, SB_HEADS, nq),
        in_specs=[
            pl.BlockSpec((tq, HEAD_W), lambda b, h, i: (b * nq + i, h)),
            _whole_seq_spec(seq, SB_HEADS), _whole_seq_spec(seq, 2 * SB_HEADS),
        ],
        out_specs=pl.BlockSpec((tq, HEAD_W), lambda b, h, i: (b * nq + i, h)),
        out_shape=jax.ShapeDtypeStruct((batch * seq, SB_HEADS * HEAD_W), BF16),
        scratch_shapes=[pltpu.VMEM((tq, HEAD_W), F32), pltpu.VMEM((tq, 1), F32),
                        pltpu.VMEM((tq, tk), F32), pltpu.VMEM((tq, tk), F32)],
        compiler_params=_cparams(("parallel", "parallel", "arbitrary")),
        name="stickbreak_attn",
    )(qkv, qkv, qkv)


def _diff_kernel(lq1_ref, lk1_ref, lq2_ref, lk2_ref, g_ref, q_ref, k_ref, v_ref, o_ref,
                 qq_ref, m_ref, l_ref, acc_ref, za_ref, zb_ref, *, tq, tk, lambda_init):
    i = pl.program_id(2)
    band = tq // tk
    q = q_ref[...]
    lane = lax.broadcasted_iota(I32, (tq, HEAD_W), 1)
    zero = jnp.zeros_like(q)
    qq_ref[0:tq, :] = jnp.where(lane < DIFF_HEAD_DIM, q, zero)
    qq_ref[tq:2 * tq, :] = jnp.where(lane >= DIFF_HEAD_DIM, q, zero)
    m_ref[...] = jnp.full(m_ref.shape, NEG_BIG, F32)
    l_ref[...] = jnp.zeros_like(l_ref)
    acc_ref[...] = jnp.zeros_like(acc_ref)

    def logits(kb):
        k0 = pl.multiple_of(kb * tk, tk)
        return _dot_nt(qq_ref[...], k_ref[pl.ds(k0, tk), :])

    def update(z, kb, rows):
        k0 = pl.multiple_of(kb * tk, tk)
        m_old = m_ref[rows, :]
        m_new = jnp.maximum(m_old, jnp.max(z, axis=1, keepdims=True))
        p = jnp.exp2(z - m_new)
        a = jnp.exp2(m_old - m_new)
        l_ref[rows, :] = a * l_ref[rows, :] + jnp.sum(p, axis=1, keepdims=True)
        acc_ref[rows, :] = a * acc_ref[rows, :] + _dot(p.astype(BF16), v_ref[pl.ds(k0, tk), :])
        m_ref[rows, :] = m_new

    for c in range(band):
        kb = i * band + c
        k0 = pl.multiple_of(kb * tk, tk)
        n = tq - c * tk
        row = lax.broadcasted_iota(I32, (n, tk), 0)
        col = lax.broadcasted_iota(I32, (n, tk), 1)
        allowed = (col // CHUNK) <= (row // CHUNK)
        for half in range(2):
            rows = slice(half * tq + c * tk, (half + 1) * tq)
            z = _dot_nt(qq_ref[rows, :], k_ref[pl.ds(k0, tk), :])
            update(jnp.where(allowed, z, NEG_BIG), kb, rows)

    below = i * band
    every = slice(0, 2 * tq)
    za_ref[...] = logits(jnp.maximum(below - 1, 0))

    def body(t, c):
        kb = below - 1 - 2 * t
        zb_ref[...] = logits(jnp.maximum(kb - 1, 0))
        update(za_ref[...], kb, every)
        za_ref[...] = logits(jnp.maximum(kb - 2, 0))
        update(zb_ref[...], kb - 1, every)
        return c

    lax.fori_loop(0, below // 2, body, 0)

    @pl.when(below % 2 == 1)
    def _():
        update(za_ref[...], 0, every)

    lam = (jnp.exp(jnp.sum(lq1_ref[...] * lk1_ref[...], axis=1, keepdims=True))
           - jnp.exp(jnp.sum(lq2_ref[...] * lk2_ref[...], axis=1, keepdims=True)) + lambda_init)
    o = acc_ref[0:tq, :] / l_ref[0:tq, :] - lam * (acc_ref[tq:2 * tq, :] / l_ref[tq:2 * tq, :])
    rms = lax.rsqrt(jnp.mean(o * o, axis=-1, keepdims=True) + LN_EPS)
    o_ref[...] = (o * rms * g_ref[...] * (1.0 - lambda_init)).astype(o_ref.dtype)


def _diff_attention(qkv, lq1, lk1, lq2, lk2, subln_g, batch, seq, lambda_init):
    tq = min(DIFF_TQ, seq)
    tk = min(DIFF_TK, seq)
    assert tq % tk == 0 and tk % CHUNK == 0
    nq = seq // tq
    base = 3 * SB_HEADS
    vec = _const_spec((1, DIFF_HEAD_DIM))
    return pl.pallas_call(
        functools.partial(_diff_kernel, tq=tq, tk=tk, lambda_init=lambda_init),
        grid=(batch, DIFF_HEADS, nq),
        in_specs=[
            vec, vec, vec, vec, _const_spec((1, HEAD_W)),
            pl.BlockSpec((tq, HEAD_W), lambda b, h, i: (b * nq + i, base + h)),
            _whole_seq_spec(seq, base + DIFF_HEADS), _whole_seq_spec(seq, base + 2 * DIFF_HEADS),
        ],
        out_specs=pl.BlockSpec((tq, HEAD_W), lambda b, h, i: (b * nq + i, h)),
        out_shape=jax.ShapeDtypeStruct((batch * seq, DIFF_HEADS * HEAD_W), BF16),
        scratch_shapes=[pltpu.VMEM((2 * tq, HEAD_W), BF16), pltpu.VMEM((2 * tq, 1), F32),
                        pltpu.VMEM((2 * tq, 1), F32), pltpu.VMEM((2 * tq, HEAD_W), F32),
                        pltpu.VMEM((2 * tq, tk), F32), pltpu.VMEM((2 * tq, tk), F32)],
        compiler_params=_cparams(("parallel", "parallel", "arbitrary")),
        name="diff_attn",
    )(lq1[None, :], lk1[None, :], lq2[None, :], lk2[None, :], subln_g[None, :], qkv, qkv, qkv)


def _proj2_ln_kernel(a_ref, b_ref, wa_ref, wb_ref, x_ref, g_ref, beta_ref, o_ref, *, alpha):
    y = _dot(a_ref[...], wa_ref[...]) + _dot(b_ref[...], wb_ref[...])
    o_ref[...] = _layer_norm(alpha * x_ref[...] + y, g_ref[...], beta_ref[...])


def _proj2_ln(a, b, wa, wb, x, g, beta, alpha):
    n, d = x.shape
    tm = min(ROW_TILE, n)
    ka, kb = a.shape[1], b.shape[1]
    row = lambda w: pl.BlockSpec((tm, w), lambda i: (i, 0))
    return pl.pallas_call(
        functools.partial(_proj2_ln_kernel, alpha=alpha),
        grid=(n // tm,),
        in_specs=[row(ka), row(kb), _const_spec((ka, d)), _const_spec((kb, d)), row(d),
                  _const_spec((1, d)), _const_spec((1, d))],
        out_specs=row(d),
        out_shape=jax.ShapeDtypeStruct((n, d), F32),
        compiler_params=_cparams(("parallel",)),
        name="attn_outproj_ln",
    )(a, b, wa, wb, x, g[None, :], beta[None, :])


def _xattn_kernel(x_ref, kv_ref, wq_ref, wo_ref, g_ref, beta_ref, o_ref, *, alpha, scale):
    x = x_ref[...]
    q = _dot(x.astype(BF16), wq_ref[...]).astype(BF16)
    width = wq_ref.shape[1]
    outs = []
    for h in range(XA_HEADS):
        qh = q[:, h * HEAD_W:(h + 1) * HEAD_W]
        kh = kv_ref[:, h * HEAD_W:(h + 1) * HEAD_W]
        vh = kv_ref[:, width + h * HEAD_W:width + (h + 1) * HEAD_W]
        z = _dot_nt(qh, kh) * scale
        z = z - jnp.max(z, axis=1, keepdims=True)
        p = jnp.exp(z)
        p = p / jnp.sum(p, axis=1, keepdims=True)
        outs.append(_dot(p.astype(BF16), vh))
    o = jnp.concatenate(outs, axis=1).astype(BF16)
    y = _dot(o, wo_ref[...])
    o_ref[...] = _layer_norm(alpha * x + y, g_ref[...], beta_ref[...])


def _xattn_ln(x, kv, wq, wo, g, beta, seq, n_mem, alpha):
    n, d = x.shape
    tm = min(ROW_TILE, seq)
    per_seq = seq // tm
    width = wq.shape[1]
    return pl.pallas_call(
        functools.partial(_xattn_kernel, alpha=alpha, scale=HEAD_W ** -0.5),
        grid=(n // tm,),
        in_specs=[pl.BlockSpec((tm, d), lambda i: (i, 0)),
                  pl.BlockSpec((n_mem, 2 * width), lambda i: (i // per_seq, 0)),
                  _const_spec((d, width)), _const_spec((width, d)),
                  _const_spec((1, d)), _const_spec((1, d))],
        out_specs=pl.BlockSpec((tm, d), lambda i: (i, 0)),
        out_shape=jax.ShapeDtypeStruct((n, d), F32),
        compiler_params=_cparams(("parallel",)),
        name="xattn_ln",
    )(x, kv, wq, wo, g[None, :], beta[None, :])


def _swiglu_acc(xb, w1_ref, w3_ref, w2_ref, rows, d_out):
    d_ff = w1_ref.shape[1]
    acc = jnp.zeros((rows, d_out), F32)
    for c in range(d_ff // FF_CHUNK):
        sl = slice(c * FF_CHUNK, (c + 1) * FF_CHUNK)
        a = _dot(xb, w1_ref[:, sl])
        b = _dot(xb, w3_ref[:, sl])
        h = (a * _sigmoid(a) * b).astype(BF16)
        acc = acc + _dot(h, w2_ref[sl, :])
    return acc


def _ffn_kernel(x_ref, w1_ref, w3_ref, w2_ref, g_ref, beta_ref, o_ref, *, alpha):
    x = x_ref[...]
    y = _swiglu_acc(x.astype(BF16), w1_ref, w3_ref, w2_ref, x.shape[0], x.shape[1])
    o_ref[...] = _layer_norm(alpha * x + y, g_ref[...], beta_ref[...])


def _ffn_ln(x, w1, w3, w2, g, beta, alpha):
    n, d = x.shape
    f = w1.shape[1]
    assert f % FF_CHUNK == 0
    tm = min(ROW_TILE, n)
    return pl.pallas_call(
        functools.partial(_ffn_kernel, alpha=alpha),
        grid=(n // tm,),
        in_specs=[pl.BlockSpec((tm, d), lambda i: (i, 0)),
                  _const_spec((d, f)), _const_spec((d, f)), _const_spec((f, d)),
                  _const_spec((1, d)), _const_spec((1, d))],
        out_specs=pl.BlockSpec((tm, d), lambda i: (i, 0)),
        out_shape=jax.ShapeDtypeStruct((n, d), F32),
        compiler_params=_cparams(("parallel",)),
        name="ffn_ln",
    )(x, w1, w3, w2, g[None, :], beta[None, :])


def _conv_kernel(x_ref, xp_ref, wpw1_ref, bpw1_ref, dww_ref, dwb_ref, cg_ref, cb_ref,
                 wpw2_ref, bpw2_ref, g_ref, beta_ref, o_ref, h_scr, c_scr, *, alpha, per_seq):
    i = pl.program_id(0)
    tm, ch = x_ref.shape

    def glu(xv):
        xb = xv.astype(BF16)
        a = _dot(xb, wpw1_ref[:, :ch]) + bpw1_ref[:, :ch]
        g = _dot(xb, wpw1_ref[:, ch:]) + bpw1_ref[:, ch:]
        return a * _sigmoid(g)

    halo = glu(xp_ref[...])
    h_scr[0:CONV_HALO, :] = jnp.where(i % per_seq == 0, 0.0, halo)
    x = x_ref[...]
    h_scr[CONV_HALO:CONV_HALO + tm, :] = glu(x)
    h_scr[CONV_HALO + tm:CONV_HALO + tm + CONV_ROWS, :] = jnp.zeros((CONV_ROWS, ch), F32)

    first = CONV_HALO - (CONV_WIDTH - 1)
    sub = lax.broadcasted_iota(I32, (CONV_ROWS, HEAD_W), 0)

    for lt in range(ch // HEAD_W):
        lanes = slice(lt * HEAD_W, (lt + 1) * HEAD_W)

        def partials(base):
            out = []
            for s in range(CONV_ROWS):
                acc = None
                for a in range((first + CONV_WIDTH - 1) // CONV_ROWS + 1):
                    k = CONV_ROWS * a + s - first
                    if 0 <= k < CONV_WIDTH:
                        r = pl.multiple_of(base + CONV_ROWS * a, CONV_ROWS)
                        term = h_scr[pl.ds(r, CONV_ROWS), lanes] * dww_ref[k:k + 1, lanes]
                        acc = term if acc is None else acc + term
                out.append(acc)
            return tuple(out)

        def rows(c, g_old):
            r0 = pl.multiple_of(c * CONV_ROWS, CONV_ROWS)
            g_new = partials(r0 + CONV_ROWS)
            acc = g_old[0]
            for s in range(1, CONV_ROWS):
                acc = acc + pltpu.roll(jnp.where(sub >= s, g_old[s], g_new[s]), CONV_ROWS - s, 0)
            c_scr[pl.ds(r0, CONV_ROWS), lanes] = acc
            return g_new

        lax.fori_loop(0, tm // CONV_ROWS, rows, partials(0))

    y = _layer_norm(c_scr[...] + dwb_ref[...], cg_ref[...], cb_ref[...])
    y = _dot((y * _sigmoid(y)).astype(BF16), wpw2_ref[...]) + bpw2_ref[...]
    o_ref[...] = _layer_norm(alpha * x + y, g_ref[...], beta_ref[...])


def _conv_ln(x, wpw1, bpw1, dww, dwb, cg, cb, wpw2, bpw2, g, beta, seq, alpha):
    n, d = x.shape
    tm = min(ROW_TILE, seq)
    per_seq = seq // tm
    halo_per_tile = tm // CONV_HALO
    vec = lambda w: _const_spec((1, w))
    return pl.pallas_call(
        functools.partial(_conv_kernel, alpha=alpha, per_seq=per_seq),
        grid=(n // tm,),
        in_specs=[pl.BlockSpec((tm, d), lambda i: (i, 0)),
                  pl.BlockSpec((CONV_HALO, d), lambda i: (jnp.maximum(i * halo_per_tile - 1, 0), 0)),
                  _const_spec((d, 2 * d)), vec(2 * d), _const_spec((CONV_WIDTH, d)), vec(d), vec(d), vec(d),
                  _const_spec((d, d)), vec(d), vec(d), vec(d)],
        out_specs=pl.BlockSpec((tm, d), lambda i: (i, 0)),
        out_shape=jax.ShapeDtypeStruct((n, d), F32),
        scratch_shapes=[pltpu.VMEM((tm + CONV_HALO + CONV_ROWS, d), F32), pltpu.VMEM((tm, d), F32)],
        compiler_params=_cparams(("parallel",)),
        name="conv_module_ln",
    )(x, x, wpw1, bpw1[None, :], dww, dwb[None, :], cg[None, :], cb[None, :],
      wpw2, bpw2[None, :], g[None, :], beta[None, :])


def _router_kernel(x_ref, w_ref, b_ref, meta_ref, before_ref, total_ref, count_scr):
    i = pl.program_id(0)
    tm = x_ref.shape[0]

    @pl.when(i == 0)
    def _():
        count_scr[...] = jnp.zeros_like(count_scr)

    logits = jnp.dot(x_ref[...], w_ref[...], preferred_element_type=F32,
                     precision=lax.Precision.HIGHEST) + b_ref[...]
    lane = lax.broadcasted_iota(I32, logits.shape, 1)
    m1 = jnp.max(logits, axis=1, keepdims=True)
    i1 = jnp.min(jnp.where(logits == m1, lane, HEAD_W), axis=1, keepdims=True)
    rest = jnp.where(lane == i1, NEG_BIG * 2, logits)
    m2 = jnp.max(rest, axis=1, keepdims=True)
    i2 = jnp.min(jnp.where(rest == m2, lane, HEAD_W), axis=1, keepdims=True)
    t = jnp.exp(m2 - m1)
    g1 = 1.0 / (1.0 + t)
    g2 = t / (1.0 + t)

    pick1 = lane == i1
    pick2 = lane == i2
    chosen = jnp.where(pick1, 1.0, jnp.where(pick2, 1.0, 0.0))
    r = lax.broadcasted_iota(I32, (tm, tm), 0)
    c = lax.broadcasted_iota(I32, (tm, tm), 1)
    earlier = jnp.where(c < r, 1.0, 0.0).astype(BF16)
    before = count_scr[0:1, :]
    rank = _dot(earlier, chosen.astype(BF16)) + before
    r1 = jnp.sum(jnp.where(pick1, rank, 0.0), axis=1, keepdims=True)
    r2 = jnp.sum(jnp.where(pick2, rank, 0.0), axis=1, keepdims=True)

    meta = jnp.where(lane == 0, i1.astype(F32), 0.0)
    meta = jnp.where(lane == 1, i2.astype(F32), meta)
    meta = jnp.where(lane == 2, r1, meta)
    meta = jnp.where(lane == 3, r2, meta)
    meta = jnp.where(lane == 4, g1, meta)
    meta = jnp.where(lane == 5, g2, meta)
    meta_ref[...] = meta

    before_ref[...] = jnp.broadcast_to(before, before_ref.shape)
    total = before + jnp.sum(chosen, axis=0, keepdims=True)
    count_scr[...] = jnp.broadcast_to(total, count_scr.shape)
    total_ref[...] = jnp.broadcast_to(total, total_ref.shape)


def _router(x, w_pad, b_pad):
    n, d = x.shape
    tm = min(ROUTER_TILE, n)
    nt = n // tm
    return pl.pallas_call(
        _router_kernel,
        grid=(nt,),
        in_specs=[pl.BlockSpec((tm, d), lambda i: (i, 0)), _const_spec((d, HEAD_W)), _const_spec((1, HEAD_W))],
        out_specs=[pl.BlockSpec((tm, HEAD_W), lambda i: (i, 0)),
                   pl.BlockSpec((8, HEAD_W), lambda i: (i, 0)),
                   _const_spec((8, HEAD_W))],
        out_shape=[jax.ShapeDtypeStruct((n, HEAD_W), F32),
                   jax.ShapeDtypeStruct((nt * 8, HEAD_W), F32),
                   jax.ShapeDtypeStruct((8, HEAD_W), F32)],
        scratch_shapes=[pltpu.VMEM((8, HEAD_W), F32)],
        compiler_params=_cparams(("arbitrary",)),
        name="moe_router",
    )(x, w_pad, b_pad)


def _dispatch_kernel(vb_ref, vt_ref, vf_ref, vv_ref, x_ref, d1_ref, d2_ref, o_ref):
    v = pl.program_id(0)
    rows, ts = o_ref.shape[0], x_ref.shape[0]

    @pl.when(vf_ref[v] == 1)
    def _():
        o_ref[...] = jnp.zeros_like(o_ref)

    @pl.when(vv_ref[v] == 1)
    def _():
        rid = vb_ref[v] * rows + lax.broadcasted_iota(I32, (rows, ts), 0)
        onehot = jnp.where(d1_ref[...] == rid, 1.0, jnp.where(d2_ref[...] == rid, 1.0, 0.0)).astype(BF16)
        o_ref[...] += _dot(onehot, x_ref[...].astype(BF16)).astype(o_ref.dtype)


def _dispatch(x, dest1, dest2, vis, n_rows):
    n, d = x.shape
    ts = min(SRC_TILE, n)
    n_src = n // ts
    n_vis = vis[0].shape[0]
    dspec = pl.BlockSpec((None, 1, ts), lambda v, vb, vt, vf, vv: (vt[v], 0, 0))
    grid_spec = pltpu.PrefetchScalarGridSpec(
        num_scalar_prefetch=4,
        grid=(n_vis,),
        in_specs=[pl.BlockSpec((ts, d), lambda v, vb, vt, vf, vv: (vt[v], 0)), dspec, dspec],
        out_specs=pl.BlockSpec((DISP_BLOCK, d), lambda v, vb, vt, vf, vv: (vb[v], 0)),
    )
    return pl.pallas_call(
        _dispatch_kernel,
        grid_spec=grid_spec,
        out_shape=jax.ShapeDtypeStruct((n_rows, d), BF16),
        compiler_params=_cparams(("arbitrary",)),
        name="moe_dispatch",
    )(*vis, x, dest1.reshape(n_src, 1, ts), dest2.reshape(n_src, 1, ts))


def _expert_kernel(be_ref, bv_ref, x_ref, w1_ref, w3_ref, w2_ref, o_ref):
    j = pl.program_id(0)

    @pl.when(bv_ref[j] == 1)
    def _():
        y = _swiglu_acc(x_ref[...], w1_ref, w3_ref, w2_ref, x_ref.shape[0], o_ref.shape[1])
        o_ref[...] = y.astype(o_ref.dtype)

    @pl.when(bv_ref[j] == 0)
    def _():
        o_ref[...] = jnp.zeros_like(o_ref)


def _experts(xs, w1, w3, w2, block_e, block_valid):
    n_rows, d = xs.shape
    f = w1.shape[2]
    assert f % FF_CHUNK == 0
    n_blocks = n_rows // EXPERT_BLOCK
    n_steps = block_e.shape[0]
    grid_spec = pltpu.PrefetchScalarGridSpec(
        num_scalar_prefetch=2,
        grid=(n_steps,),
        in_specs=[pl.BlockSpec((EXPERT_BLOCK, d), lambda j, be, bv: (jnp.minimum(j, n_blocks - 1), 0)),
                  pl.BlockSpec((None, d, f), lambda j, be, bv: (be[j], 0, 0)),
                  pl.BlockSpec((None, d, f), lambda j, be, bv: (be[j], 0, 0)),
                  pl.BlockSpec((None, f, d), lambda j, be, bv: (be[j], 0, 0))],
        out_specs=pl.BlockSpec((EXPERT_BLOCK, d), lambda j, be, bv: (j, 0)),
    )
    return pl.pallas_call(
        _expert_kernel,
        grid_spec=grid_spec,
        out_shape=jax.ShapeDtypeStruct((n_steps * EXPERT_BLOCK, d), BF16),
        compiler_params=_cparams(("arbitrary",)),
        name="moe_experts",
    )(block_e, block_valid, xs, w1, w3, w2)


def _combine_kernel(wb_ref, x_ref, cm_ref, y_hbm, g_ref, beta_ref, o_ref, win, sem, *, alpha):
    i = pl.program_id(0)
    n_steps = pl.num_programs(0)
    tm = x_ref.shape[0]

    def window_copy(step, slot, e):
        start = pl.multiple_of(wb_ref[step * N_EXPERTS + e], COMB_ALIGN)
        return pltpu.make_async_copy(y_hbm.at[pl.ds(start, COMB_WIN), :],
                                     win.at[slot, pl.ds(e * COMB_WIN, COMB_WIN), :],
                                     sem.at[slot, e])

    @pl.when(i == 0)
    def _():
        for e in range(N_EXPERTS):
            window_copy(0, 0, e).start()

    slot = i % 2

    @pl.when(i + 1 < n_steps)
    def _():
        for e in range(N_EXPERTS):
            window_copy(i + 1, 1 - slot, e).start()

    for e in range(N_EXPERTS):
        window_copy(i, slot, e).wait()

    cm = cm_ref[...]
    col = lax.broadcasted_iota(I32, (tm, N_EXPERTS * COMB_WIN), 1).astype(F32)
    p1 = jnp.where(col == cm[:, 0:1], 1.0, 0.0).astype(BF16)
    p2 = jnp.where(col == cm[:, 1:2], 1.0, 0.0).astype(BF16)
    rows = _dot(jnp.concatenate([p1, p2], axis=0), win[slot])
    y = cm[:, 2:3] * rows[:tm] + cm[:, 3:4] * rows[tm:]
    o_ref[...] = _layer_norm(alpha * x_ref[...] + y, g_ref[...], beta_ref[...])


def _combine_ln(x, cm, wbase, yb, g, beta, alpha):
    n, d = x.shape
    tm = min(COMB_TILE, n)
    grid_spec = pltpu.PrefetchScalarGridSpec(
        num_scalar_prefetch=1,
        grid=(n // tm,),
        in_specs=[pl.BlockSpec((tm, d), lambda i, wb: (i, 0)),
                  pl.BlockSpec((tm, 8), lambda i, wb: (i, 0)),
                  pl.BlockSpec(memory_space=pl.ANY),
                  pl.BlockSpec((1, d), lambda i, wb: (0, 0)),
                  pl.BlockSpec((1, d), lambda i, wb: (0, 0))],
        out_specs=pl.BlockSpec((tm, d), lambda i, wb: (i, 0)),
        scratch_shapes=[pltpu.VMEM((2, N_EXPERTS * COMB_WIN, d), BF16),
                        pltpu.SemaphoreType.DMA((2, N_EXPERTS))],
    )
    return pl.pallas_call(
        functools.partial(_combine_kernel, alpha=alpha),
        grid_spec=grid_spec,
        out_shape=jax.ShapeDtypeStruct((n, d), F32),
        compiler_params=_cparams(("arbitrary",)),
        name="moe_combine_ln",
    )(wbase, x, cm, yb, g[None, :], beta[None, :])


def _moe_ln(x, router_w, router_b, w1, w3, w2, g, beta, alpha):
    n, d = x.shape
    n_exp = router_w.shape[1]
    assert n_exp == N_EXPERTS
    w_pad = jnp.zeros((d, HEAD_W), F32).at[:, :n_exp].set(router_w)
    b_pad = jnp.full((1, HEAD_W), NEG_BIG, F32).at[0, :n_exp].set(router_b)
    meta, before, total = _router(x, w_pad, b_pad)

    e1, e2 = meta[:, 0].astype(I32), meta[:, 1].astype(I32)
    r1, r2 = meta[:, 2].astype(I32), meta[:, 3].astype(I32)
    counts = total[0, :n_exp].astype(I32)
    padded = (counts + EXPERT_BLOCK - 1) // EXPERT_BLOCK * EXPERT_BLOCK
    pend = jnp.cumsum(padded)
    pstart = pend - padded
    experts = jnp.arange(n_exp, dtype=I32)[None, :]
    pick1 = (e1[:, None] == experts).astype(I32)
    pick2 = (e2[:, None] == experts).astype(I32)
    dest1 = jnp.sum(pick1 * pstart[None, :], axis=1) + r1
    dest2 = jnp.sum(pick2 * pstart[None, :], axis=1) + r2
    n_blocks = (n * TOP_K) // EXPERT_BLOCK + n_exp
    n_rows = n_blocks * EXPERT_BLOCK

    def owner(starts):
        return jnp.minimum(jnp.sum((starts[:, None] >= pend[None, :]).astype(I32), axis=1), n_exp - 1)

    bstart = jnp.arange(n_blocks + 1, dtype=I32) * EXPERT_BLOCK
    block_e = owner(bstart)
    block_valid = (bstart < (pstart + counts)[block_e]).astype(I32)

    rt = min(ROUTER_TILE, n)
    before_rt = before[::8, :n_exp].astype(I32)
    ts = min(SRC_TILE, n)
    before_src = before_rt[:: ts // rt]
    n_src = n // ts

    n_db = n_rows // DISP_BLOCK
    dstart = jnp.arange(n_db, dtype=I32) * DISP_BLOCK
    d_e = owner(dstart)
    rs = dstart - pstart[d_e]
    re = jnp.minimum(rs + DISP_BLOCK, counts[d_e])
    d_valid = rs < counts[d_e]
    cnt_before = before_src[:, d_e].T
    t_first = jnp.sum((cnt_before <= rs[:, None]).astype(I32), axis=1) - 1
    t_last = jnp.sum((cnt_before <= (re - 1)[:, None]).astype(I32), axis=1) - 1
    t_first = jnp.where(d_valid, t_first, 0)
    t_last = jnp.where(d_valid, t_last, 0)
    n_vis = t_last - t_first + 1
    v_start = jnp.cumsum(n_vis) - n_vis
    v_total = jnp.sum(n_vis)
    max_vis = n_db + n_exp * n_src
    v = jnp.arange(max_vis, dtype=I32)
    vb = jnp.clip(jnp.searchsorted(v_start, v, side="right").astype(I32) - 1, 0, n_db - 1)
    vt = jnp.clip(t_first[vb] + v - v_start[vb], 0, n_src - 1)
    vvalid = (v < v_total).astype(I32)
    vfirst = (v == v_start[vb]).astype(I32)
    xs = _dispatch(x, dest1, dest2, (vb, vt, vfirst, vvalid), n_rows)

    yb = _experts(xs, w1, w3, w2, block_e, block_valid)

    ct = min(COMB_TILE, n)
    before_ct = before_rt[:: ct // rt]
    wbase = (pstart[None, :] + before_ct) // COMB_ALIGN * COMB_ALIGN
    wbase_tok = jnp.broadcast_to(wbase[:, None, :], (n // ct, ct, n_exp)).reshape(n, n_exp)
    c1 = e1 * COMB_WIN + dest1 - jnp.sum(pick1 * wbase_tok, axis=1)
    c2 = e2 * COMB_WIN + dest2 - jnp.sum(pick2 * wbase_tok, axis=1)
    cm = jnp.stack([c1.astype(F32), c2.astype(F32), meta[:, 4], meta[:, 5]]
                   + [jnp.zeros((n,), F32)] * 4, axis=1)
    return _combine_ln(x, cm, wbase.reshape(-1).astype(I32), yb, g, beta, alpha)


def kernel(x, mem, ln1_g, ln1_b, lnx_g, lnx_b, ln2_g, ln2_b, xa_wq, xa_wkv, xa_wo, attn_w_in, attn_w_o, diff_lq1, diff_lk1, diff_lq2, diff_lk2, diff_subln_g, ffn_w1, ffn_w3, ffn_w2, conv_w_pw1, conv_b_pw1, conv_dw_w, conv_dw_b, conv_ln_g, conv_ln_b, conv_w_pw2, conv_b_pw2, moe_router_w, moe_router_b, moe_w1, moe_w3, moe_w2):
    batch, seq, d = x.shape
    n_mem = mem.shape[1]
    depth = ln1_g.shape[0]
    alpha = (2 * depth) ** 0.25
    sb_width = SB_HEADS * HEAD_W

    bf = lambda w: w.astype(BF16)
    cos_t, sa_t, sb_t = _rope_tables(seq)
    h = x.reshape(batch * seq, d)
    mem2 = mem.reshape(batch * n_mem, d)

    for i in range(depth):
        j = i // 2
        if i % 2 == 0:
            lambda_init = 0.8 - 0.6 * math.exp(-0.3 * i)
            qkv = _inproj(h, bf(attn_w_in[j]), cos_t, sa_t, sb_t, seq)
            o_sb = _sb_attention(qkv, batch, seq)
            o_df = _diff_attention(qkv, diff_lq1[j], diff_lk1[j], diff_lq2[j], diff_lk2[j],
                                   diff_subln_g[j], batch, seq, lambda_init)
            w_o = bf(attn_w_o[j])
            h = _proj2_ln(o_sb, o_df, w_o[:sb_width], w_o[sb_width:], h, ln1_g[i], ln1_b[i], alpha)
        else:
            h = _conv_ln(h, bf(conv_w_pw1[j]), conv_b_pw1[j], conv_dw_w[j], conv_dw_b[j],
                         conv_ln_g[j], conv_ln_b[j], bf(conv_w_pw2[j]), conv_b_pw2[j],
                         ln1_g[i], ln1_b[i], seq, alpha)
        kv = _matmul(mem2, bf(xa_wkv[i]), "xattn_kv")
        h = _xattn_ln(h, kv, bf(xa_wq[i]), bf(xa_wo[i]), lnx_g[i], lnx_b[i], seq, n_mem, alpha)
        if i % 2 == 0:
            h = _ffn_ln(h, bf(ffn_w1[j]), bf(ffn_w3[j]), bf(ffn_w2[j]), ln2_g[i], ln2_b[i], alpha)
        else:
            h = _moe_ln(h, moe_router_w[j], moe_router_b[j], bf(moe_w1[j]), bf(moe_w3[j]), bf(moe_w2[j]),
                        ln2_g[i], ln2_b[i], alpha)
    return h.reshape(batch, seq, d)
```

```python
import functools
import math

import jax
import jax.numpy as jnp
from jax import lax
from jax.experimental import pallas as pl
from jax.experimental.pallas import tpu as pltpu

F32 = jnp.float32
BF16 = jnp.bfloat16
I32 = jnp.int32

LN_EPS = 1e-5
NEG_BIG = -1e30

CHUNK = 64
SB_HEADS = 4
HEAD_W = 128
DIFF_HEADS = 4
DIFF_HEAD_DIM = 64
ROPE_THETA = 10000.0
CONV_WIDTH = 31
CONV_HALO = 32
XA_HEADS = 4
N_EXPERTS = 8
TOP_K = 2

VMEM_LIMIT = 56 * 1024 * 1024
ROW_TILE = 512
SB_TQ = 1024
SB_TK = 256
SB_UNROLL = 4
DIFF_TQ = 512
DIFF_TK = 512
LOG2E = 1.4426950408889634
FF_CHUNK = 256
CONV_ROWS = 8
ROUTER_TILE = 128
SRC_TILE = 512
DISP_BLOCK = 256
EXPERT_BLOCK = 512
COMB_TILE = 128
COMB_ALIGN = 16
COMB_WIN = COMB_TILE + COMB_ALIGN


def _cparams(sem):
    return pltpu.CompilerParams(dimension_semantics=sem, vmem_limit_bytes=VMEM_LIMIT)


def _layer_norm(y, g, b):
    mu = jnp.mean(y, axis=-1, keepdims=True)
    d = y - mu
    var = jnp.mean(d * d, axis=-1, keepdims=True)
    return d * lax.rsqrt(var + LN_EPS) * g + b


def _sigmoid(x):
    return 1.0 / (1.0 + jnp.exp(-x))


def _neg_abs(x):
    bits = lax.bitcast_convert_type(x, jnp.uint32) | jnp.uint32(0x80000000)
    return lax.bitcast_convert_type(bits, F32)


def _dot(a, b):
    return jnp.dot(a, b, preferred_element_type=F32)


def _dot_nt(a, b):
    return lax.dot_general(a, b, (((1,), (1,)), ((), ())), preferred_element_type=F32)


def _const_spec(shape):
    nd = len(shape)
    return pl.BlockSpec(shape, lambda *_: (0,) * nd)


def _matmul_kernel(x_ref, w_ref, o_ref):
    o_ref[...] = _dot(x_ref[...].astype(BF16), w_ref[...]).astype(o_ref.dtype)


def _matmul(x, w, name):
    n, d = x.shape
    m = w.shape[1]
    tm = min(ROW_TILE, n)
    return pl.pallas_call(
        _matmul_kernel,
        grid=(n // tm,),
        in_specs=[pl.BlockSpec((tm, d), lambda i: (i, 0)), _const_spec((d, m))],
        out_specs=pl.BlockSpec((tm, m), lambda i: (i, 0)),
        out_shape=jax.ShapeDtypeStruct((n, m), BF16),
        compiler_params=_cparams(("parallel",)),
        name=name,
    )(x, w)


def _inproj_kernel(x_ref, w_ref, cos_ref, sa_ref, sb_ref, o_ref, *, width, sq_scale, dq_scale):
    xb = x_ref[...].astype(BF16)
    n_groups = o_ref.shape[1] // width
    for j in range(n_groups):
        acc = _dot(xb, w_ref[:, j * width:(j + 1) * width])
        if j in (3, 4):
            parts = []
            for c in range(width // HEAD_W):
                hc = acc[:, c * HEAD_W:(c + 1) * HEAD_W]
                rot = (hc * cos_ref[...]
                       + pltpu.roll(hc, HEAD_W - DIFF_HEAD_DIM // 2, 1) * sa_ref[...]
                       + pltpu.roll(hc, DIFF_HEAD_DIM // 2, 1) * sb_ref[...])
                parts.append(rot)
            acc = jnp.concatenate(parts, axis=1)
            if j == 3:
                acc = acc * dq_scale
        if j == 0:
            acc = acc * sq_scale
        o_ref[:, j * width:(j + 1) * width] = acc.astype(o_ref.dtype)


def _inproj(x, w, cos_t, sa_t, sb_t, seq):
    n, d = x.shape
    m = w.shape[1]
    tm = min(ROW_TILE, seq)
    per_seq = seq // tm
    tab = pl.BlockSpec((tm, HEAD_W), lambda i: (i % per_seq, 0))
    return pl.pallas_call(
        functools.partial(_inproj_kernel, width=m // 6, sq_scale=HEAD_W ** -0.5 * LOG2E,
                          dq_scale=DIFF_HEAD_DIM ** -0.5 * LOG2E),
        grid=(n // tm,),
        in_specs=[pl.BlockSpec((tm, d), lambda i: (i, 0)), _const_spec((d, m)), tab, tab, tab],
        out_specs=pl.BlockSpec((tm, m), lambda i: (i, 0)),
        out_shape=jax.ShapeDtypeStruct((n, m), BF16),
        compiler_params=_cparams(("parallel",)),
        name="attn_inproj",
    )(x, w, cos_t, sa_t, sb_t)


def _rope_tables(seq):
    half = DIFF_HEAD_DIM // 2
    inv = ROPE_THETA ** (-jnp.arange(0, DIFF_HEAD_DIM, 2, dtype=F32) / DIFF_HEAD_DIM)
    ang = jnp.arange(seq, dtype=F32)[:, None] * inv[None, :]
    cos, sin = jnp.cos(ang), jnp.sin(ang)
    zero = jnp.zeros_like(sin)
    reps = HEAD_W // DIFF_HEAD_DIM
    cos_t = jnp.tile(jnp.concatenate([cos, cos], axis=1), (1, reps))
    sa_t = jnp.tile(jnp.concatenate([-sin, zero], axis=1), (1, reps))
    sb_t = jnp.tile(jnp.concatenate([zero, sin], axis=1), (1, reps))
    del half
    return cos_t, sa_t, sb_t


def _sb_kernel(q_ref, k_ref, v_ref, o_ref, acc_ref, carry_ref, za_ref, zb_ref, *, tq, tk):
    i = pl.program_id(2)
    band = tq // tk
    row = lax.broadcasted_iota(I32, (tk, tk), 0)
    col = lax.broadcasted_iota(I32, (tk, tk), 1)
    suffix = jnp.where(row >= col, 1.0, 0.0).astype(BF16)

    def logits(kb):
        k0 = pl.multiple_of(kb * tk, tk)
        return _dot_nt(q_ref[...], k_ref[pl.ds(k0, tk), :])

    def rest(z, kb, rows, strict=None):
        k0 = pl.multiple_of(kb * tk, tk)
        drop = jnp.maximum(z, 0.0) + jnp.log(1.0 + jnp.exp2(_neg_abs(z))) * LOG2E
        if strict is not None:
            drop = jnp.where(strict, drop, 0.0)
        hi = drop.astype(BF16)
        lo = (drop - hi.astype(F32)).astype(BF16)
        within = _dot(hi, suffix) + _dot(lo, suffix)
        w = jnp.exp2(z - within - carry_ref[rows, :])
        if strict is not None:
            w = jnp.where(strict, w, 0.0)
        acc_ref[rows, :] += _dot(w.astype(BF16), v_ref[pl.ds(k0, tk), :])
        carry_ref[rows, :] += within[:, 0:1]

    acc_ref[...] = jnp.zeros_like(acc_ref)
    carry_ref[...] = jnp.zeros_like(carry_ref)
    strict = col < row
    for c in reversed(range(band)):
        kb = i * band + c
        k0 = pl.multiple_of(kb * tk, tk)
        diag = slice(c * tk, (c + 1) * tk)
        rest(_dot_nt(q_ref[diag, :], k_ref[pl.ds(k0, tk), :]), kb, diag, strict)
        if c + 1 < band:
            lower = slice((c + 1) * tk, tq)
            rest(_dot_nt(q_ref[lower, :], k_ref[pl.ds(k0, tk), :]), kb, lower)
    every = slice(0, tq)
    below = i * band

    zs = (za_ref, zb_ref)
    rem = below % SB_UNROLL

    def single(t, c):
        kb = below - 1 - t
        rest(logits(kb), kb, every)
        return c

    lax.fori_loop(0, rem, single, 0)
    top = below - rem
    za_ref[...] = logits(jnp.maximum(top - 1, 0))

    def body(t, c):
        kb = top - 1 - SB_UNROLL * t
        for u in range(SB_UNROLL):
            zs[(u + 1) % 2][...] = logits(jnp.maximum(kb - u - 1, 0))
            rest(zs[u % 2][...], kb - u, every)
        return c

    lax.fori_loop(0, top // SB_UNROLL, body, 0)
    o_ref[...] = acc_ref[...].astype(o_ref.dtype)


def _whole_seq_spec(seq, first):
    return pl.BlockSpec((seq, HEAD_W), lambda b, h, i: (b, first + h), pipeline_mode=pl.Buffered(1))


def _sb_attention(qkv, batch, seq):
    tq = min(SB_TQ, seq)
    tk = min(SB_TK, seq)
    assert SB_UNROLL % 2 == 0 and tq % tk == 0
    nq = seq // tq
    return pl.pallas_call(
        functools.partial(_sb_kernel, tq=tq, tk=tk),
        grid=(batch, SB_HEADS, nq),
        in_specs=[
            pl.BlockSpec((tq, HEAD_W), lambda b, h, i: (b * nq + i, h)),
            _whole_seq_spec(seq, SB_HEADS), _whole_seq_spec(seq, 2 * SB_HEADS),
        ],
        out_specs=pl.BlockSpec((tq, HEAD_W), lambda b, h, i: (b * nq + i, h)),
        out_shape=jax.ShapeDtypeStruct((batch * seq, SB_HEADS * HEAD_W), BF16),
        scratch_shapes=[pltpu.VMEM((tq, HEAD_W), F32), pltpu.VMEM((tq, 1), F32),
                        pltpu.VMEM((tq, tk), F32), pltpu.VMEM((tq, tk), F32)],
        compiler_params=_cparams(("parallel", "parallel", "arbitrary")),
        name="stickbreak_attn",
    )(qkv, qkv, qkv)


def _diff_kernel(lq1_ref, lk1_ref, lq2_ref, lk2_ref, g_ref, q_ref, k_ref, v_ref, o_ref,
                 qq_ref, m_ref, l_ref, acc_ref, za_ref, zb_ref, *, tq, tk, lambda_init):
    i = pl.program_id(2)
    q = q_ref[...]
    lane = lax.broadcasted_iota(I32, (tq, HEAD_W), 1)
    zero = jnp.zeros_like(q)
    qq_ref[0:tq, :] = jnp.where(lane < DIFF_HEAD_DIM, q, zero)
    qq_ref[tq:2 * tq, :] = jnp.where(lane >= DIFF_HEAD_DIM, q, zero)
    m_ref[...] = jnp.full(m_ref.shape, NEG_BIG, F32)
    l_ref[...] = jnp.zeros_like(l_ref)
    acc_ref[...] = jnp.zeros_like(acc_ref)

    def logits(kb):
        k0 = pl.multiple_of(kb * tk, tk)
        return _dot_nt(qq_ref[...], k_ref[pl.ds(k0, tk), :])

    ones = jnp.ones((tk, HEAD_W), BF16)
    reps = tk // HEAD_W

    def update(z, kb):
        k0 = pl.multiple_of(kb * tk, tk)
        m_old = m_ref[...]
        m_new = jnp.maximum(m_old, jnp.max(z, axis=1, keepdims=True))
        p = jnp.exp2(z - jnp.concatenate([m_new] * reps, axis=1))
        a = jnp.exp2(m_old - m_new)
        pv = _dot(p.astype(BF16), jnp.concatenate([v_ref[pl.ds(k0, tk), :], ones], axis=1))
        l_ref[...] = a * l_ref[...] + pv[:, HEAD_W:]
        acc_ref[...] = a * acc_ref[...] + pv[:, :HEAD_W]
        m_ref[...] = m_new

    row = lax.broadcasted_iota(I32, (2 * tq, tk), 0)
    col = lax.broadcasted_iota(I32, (2 * tq, tk), 1)
    q_pos = jnp.where(row >= tq, row - tq, row)
    allowed = (col // CHUNK) <= (q_pos // CHUNK)
    za_ref[...] = jnp.where(allowed, logits(i), NEG_BIG)

    def body(t, c):
        kb = i - 2 * t
        zb_ref[...] = logits(kb - 1)
        update(za_ref[...], kb)
        za_ref[...] = logits(jnp.maximum(kb - 2, 0))
        update(zb_ref[...], kb - 1)
        return c

    lax.fori_loop(0, (i + 1) // 2, body, 0)

    @pl.when(i % 2 == 0)
    def _():
        update(za_ref[...], 0)

    lam = (jnp.exp(jnp.sum(lq1_ref[...] * lk1_ref[...], axis=1, keepdims=True))
           - jnp.exp(jnp.sum(lq2_ref[...] * lk2_ref[...], axis=1, keepdims=True)) + lambda_init)
    o = acc_ref[0:tq, :] / l_ref[0:tq, :] - lam * (acc_ref[tq:2 * tq, :] / l_ref[tq:2 * tq, :])
    rms = lax.rsqrt(jnp.mean(o * o, axis=-1, keepdims=True) + LN_EPS)
    o_ref[...] = (o * rms * g_ref[...] * (1.0 - lambda_init)).astype(o_ref.dtype)


def _diff_attention(qkv, lq1, lk1, lq2, lk2, subln_g, batch, seq, lambda_init):
    tq = min(DIFF_TQ, seq)
    tk = min(DIFF_TK, seq)
    assert tq == tk and tk % CHUNK == 0
    nq = seq // tq
    base = 3 * SB_HEADS
    vec = _const_spec((1, DIFF_HEAD_DIM))
    return pl.pallas_call(
        functools.partial(_diff_kernel, tq=tq, tk=tk, lambda_init=lambda_init),
        grid=(batch, DIFF_HEADS, nq),
        in_specs=[
            vec, vec, vec, vec, _const_spec((1, HEAD_W)),
            pl.BlockSpec((tq, HEAD_W), lambda b, h, i: (b * nq + i, base + h)),
            _whole_seq_spec(seq, base + DIFF_HEADS), _whole_seq_spec(seq, base + 2 * DIFF_HEADS),
        ],
        out_specs=pl.BlockSpec((tq, HEAD_W), lambda b, h, i: (b * nq + i, h)),
        out_shape=jax.ShapeDtypeStruct((batch * seq, DIFF_HEADS * HEAD_W), BF16),
        scratch_shapes=[pltpu.VMEM((2 * tq, HEAD_W), BF16), pltpu.VMEM((2 * tq, HEAD_W), F32),
                        pltpu.VMEM((2 * tq, HEAD_W), F32), pltpu.VMEM((2 * tq, HEAD_W), F32),
                        pltpu.VMEM((2 * tq, tk), F32), pltpu.VMEM((2 * tq, tk), F32)],
        compiler_params=_cparams(("parallel", "parallel", "arbitrary")),
        name="diff_attn",
    )(lq1[None, :], lk1[None, :], lq2[None, :], lk2[None, :], subln_g[None, :], qkv, qkv, qkv)


def _proj2_ln_kernel(a_ref, b_ref, wa_ref, wb_ref, x_ref, g_ref, beta_ref, o_ref, *, alpha):
    y = _dot(a_ref[...], wa_ref[...]) + _dot(b_ref[...], wb_ref[...])
    o_ref[...] = _layer_norm(alpha * x_ref[...] + y, g_ref[...], beta_ref[...])


def _proj2_ln(a, b, wa, wb, x, g, beta, alpha):
    n, d = x.shape
    tm = min(ROW_TILE, n)
    ka, kb = a.shape[1], b.shape[1]
    row = lambda w: pl.BlockSpec((tm, w), lambda i: (i, 0))
    return pl.pallas_call(
        functools.partial(_proj2_ln_kernel, alpha=alpha),
        grid=(n // tm,),
        in_specs=[row(ka), row(kb), _const_spec((ka, d)), _const_spec((kb, d)), row(d),
                  _const_spec((1, d)), _const_spec((1, d))],
        out_specs=row(d),
        out_shape=jax.ShapeDtypeStruct((n, d), F32),
        compiler_params=_cparams(("parallel",)),
        name="attn_outproj_ln",
    )(a, b, wa, wb, x, g[None, :], beta[None, :])


def _xattn_kernel(x_ref, kv_ref, wq_ref, wo_ref, g_ref, beta_ref, o_ref, *, alpha, scale):
    x = x_ref[...]
    q = _dot(x.astype(BF16), wq_ref[...]).astype(BF16)
    width = wq_ref.shape[1]
    outs = []
    for h in range(XA_HEADS):
        qh = q[:, h * HEAD_W:(h + 1) * HEAD_W]
        kh = kv_ref[:, h * HEAD_W:(h + 1) * HEAD_W]
        vh = kv_ref[:, width + h * HEAD_W:width + (h + 1) * HEAD_W]
        z = _dot_nt(qh, kh) * scale
        z = z - jnp.max(z, axis=1, keepdims=True)
        p = jnp.exp(z)
        p = p / jnp.sum(p, axis=1, keepdims=True)
        outs.append(_dot(p.astype(BF16), vh))
    o = jnp.concatenate(outs, axis=1).astype(BF16)
    y = _dot(o, wo_ref[...])
    o_ref[...] = _layer_norm(alpha * x + y, g_ref[...], beta_ref[...])


def _xattn_ln(x, kv, wq, wo, g, beta, seq, n_mem, alpha):
    n, d = x.shape
    tm = min(ROW_TILE, seq)
    per_seq = seq // tm
    width = wq.shape[1]
    return pl.pallas_call(
        functools.partial(_xattn_kernel, alpha=alpha, scale=HEAD_W ** -0.5),
        grid=(n // tm,),
        in_specs=[pl.BlockSpec((tm, d), lambda i: (i, 0)),
                  pl.BlockSpec((n_mem, 2 * width), lambda i: (i // per_seq, 0)),
                  _const_spec((d, width)), _const_spec((width, d)),
                  _const_spec((1, d)), _const_spec((1, d))],
        out_specs=pl.BlockSpec((tm, d), lambda i: (i, 0)),
        out_shape=jax.ShapeDtypeStruct((n, d), F32),
        compiler_params=_cparams(("parallel",)),
        name="xattn_ln",
    )(x, kv, wq, wo, g[None, :], beta[None, :])


def _swiglu_acc(xb, w1_ref, w3_ref, w2_ref, rows, d_out):
    d_ff = w1_ref.shape[1]
    acc = jnp.zeros((rows, d_out), F32)
    for c in range(d_ff // FF_CHUNK):
        sl = slice(c * FF_CHUNK, (c + 1) * FF_CHUNK)
        a = _dot(xb, w1_ref[:, sl])
        b = _dot(xb, w3_ref[:, sl])
        h = (a * _sigmoid(a) * b).astype(BF16)
        acc = acc + _dot(h, w2_ref[sl, :])
    return acc


def _ffn_kernel(x_ref, w1_ref, w3_ref, w2_ref, g_ref, beta_ref, o_ref, *, alpha):
    x = x_ref[...]
    y = _swiglu_acc(x.astype(BF16), w1_ref, w3_ref, w2_ref, x.shape[0], x.shape[1])
    o_ref[...] = _layer_norm(alpha * x + y, g_ref[...], beta_ref[...])


def _ffn_ln(x, w1, w3, w2, g, beta, alpha):
    n, d = x.shape
    f = w1.shape[1]
    assert f % FF_CHUNK == 0
    tm = min(ROW_TILE, n)
    return pl.pallas_call(
        functools.partial(_ffn_kernel, alpha=alpha),
        grid=(n // tm,),
        in_specs=[pl.BlockSpec((tm, d), lambda i: (i, 0)),
                  _const_spec((d, f)), _const_spec((d, f)), _const_spec((f, d)),
                  _const_spec((1, d)), _const_spec((1, d))],
        out_specs=pl.BlockSpec((tm, d), lambda i: (i, 0)),
        out_shape=jax.ShapeDtypeStruct((n, d), F32),
        compiler_params=_cparams(("parallel",)),
        name="ffn_ln",
    )(x, w1, w3, w2, g[None, :], beta[None, :])


def _conv_kernel(x_ref, xp_ref, wpw1_ref, bpw1_ref, dww_ref, dwb_ref, cg_ref, cb_ref,
                 wpw2_ref, bpw2_ref, g_ref, beta_ref, o_ref, h_scr, c_scr, *, alpha, per_seq):
    i = pl.program_id(0)
    tm, ch = x_ref.shape

    def glu(xv):
        xb = xv.astype(BF16)
        a = _dot(xb, wpw1_ref[:, :ch]) + bpw1_ref[:, :ch]
        g = _dot(xb, wpw1_ref[:, ch:]) + bpw1_ref[:, ch:]
        return a * _sigmoid(g)

    halo = glu(xp_ref[...])
    h_scr[0:CONV_HALO, :] = jnp.where(i % per_seq == 0, 0.0, halo)
    x = x_ref[...]
    h_scr[CONV_HALO:CONV_HALO + tm, :] = glu(x)
    h_scr[CONV_HALO + tm:CONV_HALO + tm + CONV_ROWS, :] = jnp.zeros((CONV_ROWS, ch), F32)

    first = CONV_HALO - (CONV_WIDTH - 1)
    sub = lax.broadcasted_iota(I32, (CONV_ROWS, HEAD_W), 0)

    for lt in range(ch // HEAD_W):
        lanes = slice(lt * HEAD_W, (lt + 1) * HEAD_W)

        def partials(base):
            out = []
            for s in range(CONV_ROWS):
                acc = None
                for a in range((first + CONV_WIDTH - 1) // CONV_ROWS + 1):
                    k = CONV_ROWS * a + s - first
                    if 0 <= k < CONV_WIDTH:
                        r = pl.multiple_of(base + CONV_ROWS * a, CONV_ROWS)
                        term = h_scr[pl.ds(r, CONV_ROWS), lanes] * dww_ref[k:k + 1, lanes]
                        acc = term if acc is None else acc + term
                out.append(acc)
            return tuple(out)

        def rows(c, g_old):
            r0 = pl.multiple_of(c * CONV_ROWS, CONV_ROWS)
            g_new = partials(r0 + CONV_ROWS)
            acc = g_old[0]
            for s in range(1, CONV_ROWS):
                acc = acc + pltpu.roll(jnp.where(sub >= s, g_old[s], g_new[s]), CONV_ROWS - s, 0)
            c_scr[pl.ds(r0, CONV_ROWS), lanes] = acc
            return g_new

        lax.fori_loop(0, tm // CONV_ROWS, rows, partials(0))

    y = _layer_norm(c_scr[...] + dwb_ref[...], cg_ref[...], cb_ref[...])
    y = _dot((y * _sigmoid(y)).astype(BF16), wpw2_ref[...]) + bpw2_ref[...]
    o_ref[...] = _layer_norm(alpha * x + y, g_ref[...], beta_ref[...])


def _conv_ln(x, wpw1, bpw1, dww, dwb, cg, cb, wpw2, bpw2, g, beta, seq, alpha):
    n, d = x.shape
    tm = min(ROW_TILE, seq)
    per_seq = seq // tm
    halo_per_tile = tm // CONV_HALO
    vec = lambda w: _const_spec((1, w))
    return pl.pallas_call(
        functools.partial(_conv_kernel, alpha=alpha, per_seq=per_seq),
        grid=(n // tm,),
        in_specs=[pl.BlockSpec((tm, d), lambda i: (i, 0)),
                  pl.BlockSpec((CONV_HALO, d), lambda i: (jnp.maximum(i * halo_per_tile - 1, 0), 0)),
                  _const_spec((d, 2 * d)), vec(2 * d), _const_spec((CONV_WIDTH, d)), vec(d), vec(d), vec(d),
                  _const_spec((d, d)), vec(d), vec(d), vec(d)],
        out_specs=pl.BlockSpec((tm, d), lambda i: (i, 0)),
        out_shape=jax.ShapeDtypeStruct((n, d), F32),
        scratch_shapes=[pltpu.VMEM((tm + CONV_HALO + CONV_ROWS, d), F32), pltpu.VMEM((tm, d), F32)],
        compiler_params=_cparams(("parallel",)),
        name="conv_module_ln",
    )(x, x, wpw1, bpw1[None, :], dww, dwb[None, :], cg[None, :], cb[None, :],
      wpw2, bpw2[None, :], g[None, :], beta[None, :])


def _router_kernel(x_ref, w_ref, b_ref, meta_ref, before_ref, total_ref, xb_ref, count_scr):
    i = pl.program_id(0)
    tm = x_ref.shape[0]

    @pl.when(i == 0)
    def _():
        count_scr[...] = jnp.zeros_like(count_scr)

    x = x_ref[...]
    xb_ref[...] = x.astype(BF16)
    logits = jnp.dot(x, w_ref[...], preferred_element_type=F32, precision=lax.Precision.HIGHEST) + b_ref[...]
    lane = lax.broadcasted_iota(I32, logits.shape, 1)
    m1 = jnp.max(logits, axis=1, keepdims=True)
    i1 = jnp.min(jnp.where(logits == m1, lane, HEAD_W), axis=1, keepdims=True)
    rest = jnp.where(lane == i1, NEG_BIG * 2, logits)
    m2 = jnp.max(rest, axis=1, keepdims=True)
    i2 = jnp.min(jnp.where(rest == m2, lane, HEAD_W), axis=1, keepdims=True)
    t = jnp.exp(m2 - m1)
    g1 = 1.0 / (1.0 + t)
    g2 = t / (1.0 + t)

    pick1 = lane == i1
    pick2 = lane == i2
    chosen = jnp.where(pick1, 1.0, jnp.where(pick2, 1.0, 0.0))
    r = lax.broadcasted_iota(I32, (tm, tm), 0)
    c = lax.broadcasted_iota(I32, (tm, tm), 1)
    earlier = jnp.where(c < r, 1.0, 0.0).astype(BF16)
    before = count_scr[0:1, :]
    rank = _dot(earlier, chosen.astype(BF16)) + before
    r1 = jnp.sum(jnp.where(pick1, rank, 0.0), axis=1, keepdims=True)
    r2 = jnp.sum(jnp.where(pick2, rank, 0.0), axis=1, keepdims=True)

    meta = jnp.where(lane == 0, i1.astype(F32), 0.0)
    meta = jnp.where(lane == 1, i2.astype(F32), meta)
    meta = jnp.where(lane == 2, r1, meta)
    meta = jnp.where(lane == 3, r2, meta)
    meta = jnp.where(lane == 4, g1, meta)
    meta = jnp.where(lane == 5, g2, meta)
    meta_ref[...] = meta

    before_ref[...] = jnp.broadcast_to(before, before_ref.shape)
    total = before + jnp.sum(chosen, axis=0, keepdims=True)
    count_scr[...] = jnp.broadcast_to(total, count_scr.shape)
    total_ref[...] = jnp.broadcast_to(total, total_ref.shape)


def _router(x, w_pad, b_pad):
    n, d = x.shape
    tm = min(ROUTER_TILE, n)
    nt = n // tm
    return pl.pallas_call(
        _router_kernel,
        grid=(nt,),
        in_specs=[pl.BlockSpec((tm, d), lambda i: (i, 0)), _const_spec((d, HEAD_W)), _const_spec((1, HEAD_W))],
        out_specs=[pl.BlockSpec((tm, HEAD_W), lambda i: (i, 0)),
                   pl.BlockSpec((8, HEAD_W), lambda i: (i, 0)),
                   _const_spec((8, HEAD_W)),
                   pl.BlockSpec((tm, d), lambda i: (i, 0))],
        out_shape=[jax.ShapeDtypeStruct((n, HEAD_W), F32),
                   jax.ShapeDtypeStruct((nt * 8, HEAD_W), F32),
                   jax.ShapeDtypeStruct((8, HEAD_W), F32),
                   jax.ShapeDtypeStruct((n, d), BF16)],
        scratch_shapes=[pltpu.VMEM((8, HEAD_W), F32)],
        compiler_params=_cparams(("arbitrary",)),
        name="moe_router",
    )(x, w_pad, b_pad)


def _dispatch_kernel(vb_ref, vt_ref, vf_ref, vv_ref, x_ref, d1_ref, d2_ref, o_ref):
    v = pl.program_id(0)
    rows, ts = o_ref.shape[0], x_ref.shape[0]

    @pl.when(vf_ref[v] == 1)
    def _():
        o_ref[...] = jnp.zeros_like(o_ref)

    @pl.when(vv_ref[v] == 1)
    def _():
        rid = vb_ref[v] * rows + lax.broadcasted_iota(I32, (rows, ts), 0)
        onehot = jnp.where(d1_ref[...] == rid, 1.0, jnp.where(d2_ref[...] == rid, 1.0, 0.0)).astype(BF16)
        o_ref[...] += _dot(onehot, x_ref[...]).astype(o_ref.dtype)


def _dispatch(x, dest1, dest2, vis, n_rows):
    n, d = x.shape
    ts = min(SRC_TILE, n)
    n_src = n // ts
    n_vis = vis[0].shape[0]
    dspec = pl.BlockSpec((None, 1, ts), lambda v, vb, vt, vf, vv: (vt[v], 0, 0))
    grid_spec = pltpu.PrefetchScalarGridSpec(
        num_scalar_prefetch=4,
        grid=(n_vis,),
        in_specs=[pl.BlockSpec((ts, d), lambda v, vb, vt, vf, vv: (vt[v], 0)), dspec, dspec],
        out_specs=pl.BlockSpec((DISP_BLOCK, d), lambda v, vb, vt, vf, vv: (vb[v], 0)),
    )
    return pl.pallas_call(
        _dispatch_kernel,
        grid_spec=grid_spec,
        out_shape=jax.ShapeDtypeStruct((n_rows, d), BF16),
        compiler_params=_cparams(("arbitrary",)),
        name="moe_dispatch",
    )(*vis, x, dest1.reshape(n_src, 1, ts), dest2.reshape(n_src, 1, ts))


def _expert_kernel(be_ref, bv_ref, x_ref, w1_ref, w3_ref, w2_ref, o_ref):
    j = pl.program_id(0)

    @pl.when(bv_ref[j] == 1)
    def _():
        y = _swiglu_acc(x_ref[...], w1_ref, w3_ref, w2_ref, x_ref.shape[0], o_ref.shape[1])
        o_ref[...] = y.astype(o_ref.dtype)

    @pl.when(bv_ref[j] == 0)
    def _():
        o_ref[...] = jnp.zeros_like(o_ref)


def _experts(xs, w1, w3, w2, block_e, block_valid):
    n_rows, d = xs.shape
    f = w1.shape[2]
    assert f % FF_CHUNK == 0
    n_blocks = n_rows // EXPERT_BLOCK
    n_steps = block_e.shape[0]
    grid_spec = pltpu.PrefetchScalarGridSpec(
        num_scalar_prefetch=2,
        grid=(n_steps,),
        in_specs=[pl.BlockSpec((EXPERT_BLOCK, d), lambda j, be, bv: (jnp.minimum(j, n_blocks - 1), 0)),
                  pl.BlockSpec((None, d, f), lambda j, be, bv: (be[j], 0, 0)),
                  pl.BlockSpec((None, d, f), lambda j, be, bv: (be[j], 0, 0)),
                  pl.BlockSpec((None, f, d), lambda j, be, bv: (be[j], 0, 0))],
        out_specs=pl.BlockSpec((EXPERT_BLOCK, d), lambda j, be, bv: (j, 0)),
    )
    return pl.pallas_call(
        _expert_kernel,
        grid_spec=grid_spec,
        out_shape=jax.ShapeDtypeStruct((n_steps * EXPERT_BLOCK, d), BF16),
        compiler_params=_cparams(("arbitrary",)),
        name="moe_experts",
    )(block_e, block_valid, xs, w1, w3, w2)


def _combine_kernel(wb_ref, x_ref, cm_ref, y_hbm, g_ref, beta_ref, o_ref, win, sem, *, alpha):
    i = pl.program_id(0)
    n_steps = pl.num_programs(0)
    tm = x_ref.shape[0]

    def window_copy(step, slot, e):
        start = pl.multiple_of(wb_ref[step * N_EXPERTS + e], COMB_ALIGN)
        return pltpu.make_async_copy(y_hbm.at[pl.ds(start, COMB_WIN), :],
                                     win.at[slot, pl.ds(e * COMB_WIN, COMB_WIN), :],
                                     sem.at[slot, e])

    @pl.when(i == 0)
    def _():
        for e in range(N_EXPERTS):
            window_copy(0, 0, e).start()

    slot = i % 2

    @pl.when(i + 1 < n_steps)
    def _():
        for e in range(N_EXPERTS):
            window_copy(i + 1, 1 - slot, e).start()

    for e in range(N_EXPERTS):
        window_copy(i, slot, e).wait()

    cm = cm_ref[...]
    col = lax.broadcasted_iota(I32, (tm, N_EXPERTS * COMB_WIN), 1).astype(F32)
    p1 = jnp.where(col == cm[:, 0:1], 1.0, 0.0).astype(BF16)
    p2 = jnp.where(col == cm[:, 1:2], 1.0, 0.0).astype(BF16)
    rows = _dot(jnp.concatenate([p1, p2], axis=0), win[slot])
    y = cm[:, 2:3] * rows[:tm] + cm[:, 3:4] * rows[tm:]
    o_ref[...] = _layer_norm(alpha * x_ref[...] + y, g_ref[...], beta_ref[...])


def _combine_ln(x, cm, wbase, yb, g, beta, alpha):
    n, d = x.shape
    tm = min(COMB_TILE, n)
    grid_spec = pltpu.PrefetchScalarGridSpec(
        num_scalar_prefetch=1,
        grid=(n // tm,),
        in_specs=[pl.BlockSpec((tm, d), lambda i, wb: (i, 0)),
                  pl.BlockSpec((tm, 8), lambda i, wb: (i, 0)),
                  pl.BlockSpec(memory_space=pl.ANY),
                  pl.BlockSpec((1, d), lambda i, wb: (0, 0)),
                  pl.BlockSpec((1, d), lambda i, wb: (0, 0))],
        out_specs=pl.BlockSpec((tm, d), lambda i, wb: (i, 0)),
        scratch_shapes=[pltpu.VMEM((2, N_EXPERTS * COMB_WIN, d), BF16),
                        pltpu.SemaphoreType.DMA((2, N_EXPERTS))],
    )
    return pl.pallas_call(
        functools.partial(_combine_kernel, alpha=alpha),
        grid_spec=grid_spec,
        out_shape=jax.ShapeDtypeStruct((n, d), F32),
        compiler_params=_cparams(("arbitrary",)),
        name="moe_combine_ln",
    )(wbase, x, cm, yb, g[None, :], beta[None, :])


def _moe_ln(x, router_w, router_b, w1, w3, w2, g, beta, alpha):
    n, d = x.shape
    n_exp = router_w.shape[1]
    assert n_exp == N_EXPERTS
    w_pad = jnp.zeros((d, HEAD_W), F32).at[:, :n_exp].set(router_w)
    b_pad = jnp.full((1, HEAD_W), NEG_BIG, F32).at[0, :n_exp].set(router_b)
    meta, before, total, xb = _router(x, w_pad, b_pad)

    e1, e2 = meta[:, 0].astype(I32), meta[:, 1].astype(I32)
    r1, r2 = meta[:, 2].astype(I32), meta[:, 3].astype(I32)
    counts = total[0, :n_exp].astype(I32)
    padded = (counts + EXPERT_BLOCK - 1) // EXPERT_BLOCK * EXPERT_BLOCK
    pend = jnp.cumsum(padded)
    pstart = pend - padded
    experts = jnp.arange(n_exp, dtype=I32)[None, :]
    pick1 = (e1[:, None] == experts).astype(I32)
    pick2 = (e2[:, None] == experts).astype(I32)
    dest1 = jnp.sum(pick1 * pstart[None, :], axis=1) + r1
    dest2 = jnp.sum(pick2 * pstart[None, :], axis=1) + r2
    n_blocks = (n * TOP_K) // EXPERT_BLOCK + n_exp
    n_rows = n_blocks * EXPERT_BLOCK

    def owner(starts):
        return jnp.minimum(jnp.sum((starts[:, None] >= pend[None, :]).astype(I32), axis=1), n_exp - 1)

    bstart = jnp.arange(n_blocks + 1, dtype=I32) * EXPERT_BLOCK
    block_e = owner(bstart)
    block_valid = (bstart < (pstart + counts)[block_e]).astype(I32)

    rt = min(ROUTER_TILE, n)
    before_rt = before[::8, :n_exp].astype(I32)
    ts = min(SRC_TILE, n)
    before_src = before_rt[:: ts // rt]
    n_src = n // ts

    n_db = n_rows // DISP_BLOCK
    dstart = jnp.arange(n_db, dtype=I32) * DISP_BLOCK
    d_e = owner(dstart)
    rs = dstart - pstart[d_e]
    re = jnp.minimum(rs + DISP_BLOCK, counts[d_e])
    d_valid = rs < counts[d_e]
    cnt_before = before_src[:, d_e].T
    t_first = jnp.sum((cnt_before <= rs[:, None]).astype(I32), axis=1) - 1
    t_last = jnp.sum((cnt_before <= (re - 1)[:, None]).astype(I32), axis=1) - 1
    t_first = jnp.where(d_valid, t_first, 0)
    t_last = jnp.where(d_valid, t_last, 0)
    n_vis = t_last - t_first + 1
    v_start = jnp.cumsum(n_vis) - n_vis
    v_total = jnp.sum(n_vis)
    max_vis = n_db + n_exp * n_src
    v = jnp.arange(max_vis, dtype=I32)
    vb = jnp.clip(jnp.searchsorted(v_start, v, side="right").astype(I32) - 1, 0, n_db - 1)
    vt = jnp.clip(t_first[vb] + v - v_start[vb], 0, n_src - 1)
    vvalid = (v < v_total).astype(I32)
    vfirst = (v == v_start[vb]).astype(I32)
    xs = _dispatch(xb, dest1, dest2, (vb, vt, vfirst, vvalid), n_rows)

    yb = _experts(xs, w1, w3, w2, block_e, block_valid)

    ct = min(COMB_TILE, n)
    before_ct = before_rt[:: ct // rt]
    wbase = (pstart[None, :] + before_ct) // COMB_ALIGN * COMB_ALIGN
    wbase_tok = jnp.broadcast_to(wbase[:, None, :], (n // ct, ct, n_exp)).reshape(n, n_exp)
    c1 = e1 * COMB_WIN + dest1 - jnp.sum(pick1 * wbase_tok, axis=1)
    c2 = e2 * COMB_WIN + dest2 - jnp.sum(pick2 * wbase_tok, axis=1)
    cm = jnp.stack([c1.astype(F32), c2.astype(F32), meta[:, 4], meta[:, 5]]
                   + [jnp.zeros((n,), F32)] * 4, axis=1)
    return _combine_ln(x, cm, wbase.reshape(-1).astype(I32), yb, g, beta, alpha)


def kernel(x, mem, ln1_g, ln1_b, lnx_g, lnx_b, ln2_g, ln2_b, xa_wq, xa_wkv, xa_wo, attn_w_in, attn_w_o, diff_lq1, diff_lk1, diff_lq2, diff_lk2, diff_subln_g, ffn_w1, ffn_w3, ffn_w2, conv_w_pw1, conv_b_pw1, conv_dw_w, conv_dw_b, conv_ln_g, conv_ln_b, conv_w_pw2, conv_b_pw2, moe_router_w, moe_router_b, moe_w1, moe_w3, moe_w2):
    batch, seq, d = x.shape
    n_mem = mem.shape[1]
    depth = ln1_g.shape[0]
    alpha = (2 * depth) ** 0.25
    sb_width = SB_HEADS * HEAD_W

    bf = lambda w: w.astype(BF16)
    cos_t, sa_t, sb_t = _rope_tables(seq)
    h = x.reshape(batch * seq, d)
    mem2 = mem.reshape(batch * n_mem, d)

    for i in range(depth):
        j = i // 2
        if i % 2 == 0:
            lambda_init = 0.8 - 0.6 * math.exp(-0.3 * i)
            qkv = _inproj(h, bf(attn_w_in[j]), cos_t, sa_t, sb_t, seq)
            o_sb = _sb_attention(qkv, batch, seq)
            o_df = _diff_attention(qkv, diff_lq1[j], diff_lk1[j], diff_lq2[j], diff_lk2[j],
                                   diff_subln_g[j], batch, seq, lambda_init)
            w_o = bf(attn_w_o[j])
            h = _proj2_ln(o_sb, o_df, w_o[:sb_width], w_o[sb_width:], h, ln1_g[i], ln1_b[i], alpha)
        else:
            h = _conv_ln(h, bf(conv_w_pw1[j]), conv_b_pw1[j], conv_dw_w[j], conv_dw_b[j],
                         conv_ln_g[j], conv_ln_b[j], bf(conv_w_pw2[j]), conv_b_pw2[j],
                         ln1_g[i], ln1_b[i], seq, alpha)
        kv = _matmul(mem2, bf(xa_wkv[i]), "xattn_kv")
        h = _xattn_ln(h, kv, bf(xa_wq[i]), bf(xa_wo[i]), lnx_g[i], lnx_b[i], seq, n_mem, alpha)
        if i % 2 == 0:
            h = _ffn_ln(h, bf(ffn_w1[j]), bf(ffn_w3[j]), bf(ffn_w2[j]), ln2_g[i], ln2_b[i], alpha)
        else:
            h = _moe_ln(h, moe_router_w[j], moe_router_b[j], bf(moe_w1[j]), bf(moe_w3[j]), bf(moe_w2[j]),
                        ln2_g[i], ln2_b[i], alpha)
    return h.reshape(batch, seq, d)
```

```python
import functools
import math

import jax
import jax.numpy as jnp
from jax import lax
from jax.experimental import pallas as pl
from jax.experimental.pallas import tpu as pltpu

F32 = jnp.float32
BF16 = jnp.bfloat16
I32 = jnp.int32

LN_EPS = 1e-5
NEG_BIG = -1e30

CHUNK = 64
SB_HEADS = 4
HEAD_W = 128
DIFF_HEADS = 4
DIFF_HEAD_DIM = 64
ROPE_THETA = 10000.0
CONV_WIDTH = 31
CONV_HALO = 32
XA_HEADS = 4
N_EXPERTS = 8
TOP_K = 2

VMEM_LIMIT = 56 * 1024 * 1024
ROW_TILE = 512
SB_TQ = 1024
SB_TK = 256
SB_UNROLL = 4
DIFF_TQ = 512
DIFF_TK = 512
LOG2E = 1.4426950408889634
FF_CHUNK = 256
CONV_ROWS = 8
ROUTER_TILE = 512
ROUTER_ROWS = 16
SRC_TILE = 512
SEG_ALIGN = 16
EXPERT_BLOCK = 512
COMB_TILE = 128
COMB_ALIGN = 16
COMB_WIN = COMB_TILE + COMB_ALIGN


def _cparams(sem):
    return pltpu.CompilerParams(dimension_semantics=sem, vmem_limit_bytes=VMEM_LIMIT)


def _layer_norm(y, g, b):
    mu = jnp.mean(y, axis=-1, keepdims=True)
    d = y - mu
    var = jnp.mean(d * d, axis=-1, keepdims=True)
    return d * lax.rsqrt(var + LN_EPS) * g + b


def _sigmoid(x):
    return 1.0 / (1.0 + jnp.exp(-x))


def _neg_abs(x):
    bits = lax.bitcast_convert_type(x, jnp.uint32) | jnp.uint32(0x80000000)
    return lax.bitcast_convert_type(bits, F32)


def _dot(a, b):
    return jnp.dot(a, b, preferred_element_type=F32)


def _dot_nt(a, b):
    return lax.dot_general(a, b, (((1,), (1,)), ((), ())), preferred_element_type=F32)


def _const_spec(shape):
    nd = len(shape)
    return pl.BlockSpec(shape, lambda *_: (0,) * nd)


def _matmul_kernel(x_ref, w_ref, o_ref):
    o_ref[...] = _dot(x_ref[...].astype(BF16), w_ref[...]).astype(o_ref.dtype)


def _matmul(x, w, name):
    n, d = x.shape
    m = w.shape[1]
    tm = min(ROW_TILE, n)
    return pl.pallas_call(
        _matmul_kernel,
        grid=(n // tm,),
        in_specs=[pl.BlockSpec((tm, d), lambda i: (i, 0)), _const_spec((d, m))],
        out_specs=pl.BlockSpec((tm, m), lambda i: (i, 0)),
        out_shape=jax.ShapeDtypeStruct((n, m), BF16),
        compiler_params=_cparams(("parallel",)),
        name=name,
    )(x, w)


def _inproj_kernel(x_ref, w_ref, cos_ref, sa_ref, sb_ref, o_ref, *, width, sq_scale, dq_scale):
    xb = x_ref[...].astype(BF16)
    n_groups = o_ref.shape[1] // width
    for j in range(n_groups):
        acc = _dot(xb, w_ref[:, j * width:(j + 1) * width])
        if j in (3, 4):
            parts = []
            for c in range(width // HEAD_W):
                hc = acc[:, c * HEAD_W:(c + 1) * HEAD_W]
                rot = (hc * cos_ref[...]
                       + pltpu.roll(hc, HEAD_W - DIFF_HEAD_DIM // 2, 1) * sa_ref[...]
                       + pltpu.roll(hc, DIFF_HEAD_DIM // 2, 1) * sb_ref[...])
                parts.append(rot)
            acc = jnp.concatenate(parts, axis=1)
            if j == 3:
                acc = acc * dq_scale
        if j == 0:
            acc = acc * sq_scale
        o_ref[:, j * width:(j + 1) * width] = acc.astype(o_ref.dtype)


def _inproj(x, w, cos_t, sa_t, sb_t, seq):
    n, d = x.shape
    m = w.shape[1]
    tm = min(ROW_TILE, seq)
    per_seq = seq // tm
    tab = pl.BlockSpec((tm, HEAD_W), lambda i: (i % per_seq, 0))
    return pl.pallas_call(
        functools.partial(_inproj_kernel, width=m // 6, sq_scale=HEAD_W ** -0.5 * LOG2E,
                          dq_scale=DIFF_HEAD_DIM ** -0.5 * LOG2E),
        grid=(n // tm,),
        in_specs=[pl.BlockSpec((tm, d), lambda i: (i, 0)), _const_spec((d, m)), tab, tab, tab],
        out_specs=pl.BlockSpec((tm, m), lambda i: (i, 0)),
        out_shape=jax.ShapeDtypeStruct((n, m), BF16),
        compiler_params=_cparams(("parallel",)),
        name="attn_inproj",
    )(x, w, cos_t, sa_t, sb_t)


def _rope_tables(seq):
    half = DIFF_HEAD_DIM // 2
    inv = ROPE_THETA ** (-jnp.arange(0, DIFF_HEAD_DIM, 2, dtype=F32) / DIFF_HEAD_DIM)
    ang = jnp.arange(seq, dtype=F32)[:, None] * inv[None, :]
    cos, sin = jnp.cos(ang), jnp.sin(ang)
    zero = jnp.zeros_like(sin)
    reps = HEAD_W // DIFF_HEAD_DIM
    cos_t = jnp.tile(jnp.concatenate([cos, cos], axis=1), (1, reps))
    sa_t = jnp.tile(jnp.concatenate([-sin, zero], axis=1), (1, reps))
    sb_t = jnp.tile(jnp.concatenate([zero, sin], axis=1), (1, reps))
    del half
    return cos_t, sa_t, sb_t


def _sb_kernel(q_ref, k_ref, v_ref, o_ref, acc_ref, carry_ref, za_ref, zb_ref, *, tq, tk):
    i = pl.program_id(2)
    band = tq // tk
    row = lax.broadcasted_iota(I32, (tk, tk), 0)
    col = lax.broadcasted_iota(I32, (tk, tk), 1)
    suffix = jnp.where(row >= col, 1.0, 0.0).astype(BF16)

    def logits(kb):
        k0 = pl.multiple_of(kb * tk, tk)
        return _dot_nt(q_ref[...], k_ref[pl.ds(k0, tk), :])

    def rest(z, kb, rows, strict=None):
        k0 = pl.multiple_of(kb * tk, tk)
        drop = jnp.maximum(z, 0.0) + jnp.log(1.0 + jnp.exp2(_neg_abs(z))) * LOG2E
        if strict is not None:
            drop = jnp.where(strict, drop, 0.0)
        hi = drop.astype(BF16)
        lo = (drop - hi.astype(F32)).astype(BF16)
        within = _dot(hi, suffix) + _dot(lo, suffix)
        w = jnp.exp2(z - within - carry_ref[rows, :])
        if strict is not None:
            w = jnp.where(strict, w, 0.0)
        acc_ref[rows, :] += _dot(w.astype(BF16), v_ref[pl.ds(k0, tk), :])
        carry_ref[rows, :] += within[:, 0:1]

    acc_ref[...] = jnp.zeros_like(acc_ref)
    carry_ref[...] = jnp.zeros_like(carry_ref)
    strict = col < row
    for c in reversed(range(band)):
        kb = i * band + c
        k0 = pl.multiple_of(kb * tk, tk)
        diag = slice(c * tk, (c + 1) * tk)
        rest(_dot_nt(q_ref[diag, :], k_ref[pl.ds(k0, tk), :]), kb, diag, strict)
        if c + 1 < band:
            lower = slice((c + 1) * tk, tq)
            rest(_dot_nt(q_ref[lower, :], k_ref[pl.ds(k0, tk), :]), kb, lower)
    every = slice(0, tq)
    below = i * band

    zs = (za_ref, zb_ref)
    rem = below % SB_UNROLL

    def single(t, c):
        kb = below - 1 - t
        rest(logits(kb), kb, every)
        return c

    lax.fori_loop(0, rem, single, 0)
    top = below - rem
    za_ref[...] = logits(jnp.maximum(top - 1, 0))

    def body(t, c):
        kb = top - 1 - SB_UNROLL * t
        for u in range(SB_UNROLL):
            zs[(u + 1) % 2][...] = logits(jnp.maximum(kb - u - 1, 0))
            rest(zs[u % 2][...], kb - u, every)
        return c

    lax.fori_loop(0, top // SB_UNROLL, body, 0)
    o_ref[...] = acc_ref[...].astype(o_ref.dtype)


def _whole_seq_spec(seq, first):
    return pl.BlockSpec((seq, HEAD_W), lambda b, h, i: (b, first + h), pipeline_mode=pl.Buffered(1))


def _sb_attention(qkv, batch, seq):
    tq = min(SB_TQ, seq)
    tk = min(SB_TK, seq)
    assert SB_UNROLL % 2 == 0 and tq % tk == 0
    nq = seq // tq
    return pl.pallas_call(
        functools.partial(_sb_kernel, tq=tq, tk=tk),
        grid=(batch, SB_HEADS, nq),
        in_specs=[
            pl.BlockSpec((tq, HEAD_W), lambda b, h, i: (b * nq + i, h)),
            _whole_seq_spec(seq, SB_HEADS), _whole_seq_spec(seq, 2 * SB_HEADS),
        ],
        out_specs=pl.BlockSpec((tq, HEAD_W), lambda b, h, i: (b * nq + i, h)),
        out_shape=jax.ShapeDtypeStruct((batch * seq, SB_HEADS * HEAD_W), BF16),
        scratch_shapes=[pltpu.VMEM((tq, HEAD_W), F32), pltpu.VMEM((tq, 1), F32),
                        pltpu.VMEM((tq, tk), F32), pltpu.VMEM((tq, tk), F32)],
        compiler_params=_cparams(("parallel", "parallel", "arbitrary")),
        name="stickbreak_attn",
    )(qkv, qkv, qkv)


def _diff_kernel(lq1_ref, lk1_ref, lq2_ref, lk2_ref, g_ref, q_ref, k_ref, v_ref, o_ref,
                 qq_ref, m_ref, l_ref, acc_ref, za_ref, zb_ref, *, tq, tk, lambda_init):
    i = pl.program_id(2)
    q = q_ref[...]
    lane = lax.broadcasted_iota(I32, (tq, HEAD_W), 1)
    zero = jnp.zeros_like(q)
    qq_ref[0:tq, :] = jnp.where(lane < DIFF_HEAD_DIM, q, zero)
    qq_ref[tq:2 * tq, :] = jnp.where(lane >= DIFF_HEAD_DIM, q, zero)
    m_ref[...] = jnp.full(m_ref.shape, NEG_BIG, F32)
    l_ref[...] = jnp.zeros_like(l_ref)
    acc_ref[...] = jnp.zeros_like(acc_ref)

    def logits(kb):
        k0 = pl.multiple_of(kb * tk, tk)
        return _dot_nt(qq_ref[...], k_ref[pl.ds(k0, tk), :])

    ones = jnp.ones((tk, HEAD_W), BF16)
    reps = tk // HEAD_W

    def update(z, kb):
        k0 = pl.multiple_of(kb * tk, tk)
        m_old = m_ref[...]
        m_new = jnp.maximum(m_old, jnp.max(z, axis=1, keepdims=True))
        p = jnp.exp2(z - jnp.concatenate([m_new] * reps, axis=1))
        a = jnp.exp2(m_old - m_new)
        pv = _dot(p.astype(BF16), jnp.concatenate([v_ref[pl.ds(k0, tk), :], ones], axis=1))
        l_ref[...] = a * l_ref[...] + pv[:, HEAD_W:]
        acc_ref[...] = a * acc_ref[...] + pv[:, :HEAD_W]
        m_ref[...] = m_new

    row = lax.broadcasted_iota(I32, (2 * tq, tk), 0)
    col = lax.broadcasted_iota(I32, (2 * tq, tk), 1)
    q_pos = jnp.where(row >= tq, row - tq, row)
    allowed = (col // CHUNK) <= (q_pos // CHUNK)
    za_ref[...] = jnp.where(allowed, logits(i), NEG_BIG)

    def body(t, c):
        kb = i - 2 * t
        zb_ref[...] = logits(kb - 1)
        update(za_ref[...], kb)
        za_ref[...] = logits(jnp.maximum(kb - 2, 0))
        update(zb_ref[...], kb - 1)
        return c

    lax.fori_loop(0, (i + 1) // 2, body, 0)

    @pl.when(i % 2 == 0)
    def _():
        update(za_ref[...], 0)

    lam = (jnp.exp(jnp.sum(lq1_ref[...] * lk1_ref[...], axis=1, keepdims=True))
           - jnp.exp(jnp.sum(lq2_ref[...] * lk2_ref[...], axis=1, keepdims=True)) + lambda_init)
    o = acc_ref[0:tq, :] / l_ref[0:tq, :] - lam * (acc_ref[tq:2 * tq, :] / l_ref[tq:2 * tq, :])
    rms = lax.rsqrt(jnp.mean(o * o, axis=-1, keepdims=True) + LN_EPS)
    o_ref[...] = (o * rms * g_ref[...] * (1.0 - lambda_init)).astype(o_ref.dtype)


def _diff_attention(qkv, lq1, lk1, lq2, lk2, subln_g, batch, seq, lambda_init):
    tq = min(DIFF_TQ, seq)
    tk = min(DIFF_TK, seq)
    assert tq == tk and tk % CHUNK == 0
    nq = seq // tq
    base = 3 * SB_HEADS
    vec = _const_spec((1, DIFF_HEAD_DIM))
    return pl.pallas_call(
        functools.partial(_diff_kernel, tq=tq, tk=tk, lambda_init=lambda_init),
        grid=(batch, DIFF_HEADS, nq),
        in_specs=[
            vec, vec, vec, vec, _const_spec((1, HEAD_W)),
            pl.BlockSpec((tq, HEAD_W), lambda b, h, i: (b * nq + i, base + h)),
            _whole_seq_spec(seq, base + DIFF_HEADS), _whole_seq_spec(seq, base + 2 * DIFF_HEADS),
        ],
        out_specs=pl.BlockSpec((tq, HEAD_W), lambda b, h, i: (b * nq + i, h)),
        out_shape=jax.ShapeDtypeStruct((batch * seq, DIFF_HEADS * HEAD_W), BF16),
        scratch_shapes=[pltpu.VMEM((2 * tq, HEAD_W), BF16), pltpu.VMEM((2 * tq, HEAD_W), F32),
                        pltpu.VMEM((2 * tq, HEAD_W), F32), pltpu.VMEM((2 * tq, HEAD_W), F32),
                        pltpu.VMEM((2 * tq, tk), F32), pltpu.VMEM((2 * tq, tk), F32)],
        compiler_params=_cparams(("parallel", "parallel", "arbitrary")),
        name="diff_attn",
    )(lq1[None, :], lk1[None, :], lq2[None, :], lk2[None, :], subln_g[None, :], qkv, qkv, qkv)


def _proj2_ln_kernel(a_ref, b_ref, wa_ref, wb_ref, x_ref, g_ref, beta_ref, o_ref, *, alpha):
    y = _dot(a_ref[...], wa_ref[...]) + _dot(b_ref[...], wb_ref[...])
    o_ref[...] = _layer_norm(alpha * x_ref[...] + y, g_ref[...], beta_ref[...])


def _proj2_ln(a, b, wa, wb, x, g, beta, alpha):
    n, d = x.shape
    tm = min(ROW_TILE, n)
    ka, kb = a.shape[1], b.shape[1]
    row = lambda w: pl.BlockSpec((tm, w), lambda i: (i, 0))
    return pl.pallas_call(
        functools.partial(_proj2_ln_kernel, alpha=alpha),
        grid=(n // tm,),
        in_specs=[row(ka), row(kb), _const_spec((ka, d)), _const_spec((kb, d)), row(d),
                  _const_spec((1, d)), _const_spec((1, d))],
        out_specs=row(d),
        out_shape=jax.ShapeDtypeStruct((n, d), F32),
        compiler_params=_cparams(("parallel",)),
        name="attn_outproj_ln",
    )(a, b, wa, wb, x, g[None, :], beta[None, :])


def _xattn_kernel(x_ref, kv_ref, wq_ref, wo_ref, g_ref, beta_ref, o_ref, *, alpha, scale):
    x = x_ref[...]
    q = _dot(x.astype(BF16), wq_ref[...]).astype(BF16)
    width = wq_ref.shape[1]
    outs = []
    for h in range(XA_HEADS):
        qh = q[:, h * HEAD_W:(h + 1) * HEAD_W]
        kh = kv_ref[:, h * HEAD_W:(h + 1) * HEAD_W]
        vh = kv_ref[:, width + h * HEAD_W:width + (h + 1) * HEAD_W]
        z = _dot_nt(qh, kh) * scale
        z = z - jnp.max(z, axis=1, keepdims=True)
        p = jnp.exp(z)
        p = p / jnp.sum(p, axis=1, keepdims=True)
        outs.append(_dot(p.astype(BF16), vh))
    o = jnp.concatenate(outs, axis=1).astype(BF16)
    y = _dot(o, wo_ref[...])
    o_ref[...] = _layer_norm(alpha * x + y, g_ref[...], beta_ref[...])


def _xattn_ln(x, kv, wq, wo, g, beta, seq, n_mem, alpha):
    n, d = x.shape
    tm = min(ROW_TILE, seq)
    per_seq = seq // tm
    width = wq.shape[1]
    return pl.pallas_call(
        functools.partial(_xattn_kernel, alpha=alpha, scale=HEAD_W ** -0.5),
        grid=(n // tm,),
        in_specs=[pl.BlockSpec((tm, d), lambda i: (i, 0)),
                  pl.BlockSpec((n_mem, 2 * width), lambda i: (i // per_seq, 0)),
                  _const_spec((d, width)), _const_spec((width, d)),
                  _const_spec((1, d)), _const_spec((1, d))],
        out_specs=pl.BlockSpec((tm, d), lambda i: (i, 0)),
        out_shape=jax.ShapeDtypeStruct((n, d), F32),
        compiler_params=_cparams(("parallel",)),
        name="xattn_ln",
    )(x, kv, wq, wo, g[None, :], beta[None, :])


def _swiglu_acc(xb, w1_ref, w3_ref, w2_ref, rows, d_out):
    d_ff = w1_ref.shape[1]
    acc = jnp.zeros((rows, d_out), F32)
    for c in range(d_ff // FF_CHUNK):
        sl = slice(c * FF_CHUNK, (c + 1) * FF_CHUNK)
        a = _dot(xb, w1_ref[:, sl])
        b = _dot(xb, w3_ref[:, sl])
        h = (a * _sigmoid(a) * b).astype(BF16)
        acc = acc + _dot(h, w2_ref[sl, :])
    return acc


def _ffn_kernel(x_ref, w1_ref, w3_ref, w2_ref, g_ref, beta_ref, o_ref, *, alpha):
    x = x_ref[...]
    y = _swiglu_acc(x.astype(BF16), w1_ref, w3_ref, w2_ref, x.shape[0], x.shape[1])
    o_ref[...] = _layer_norm(alpha * x + y, g_ref[...], beta_ref[...])


def _ffn_ln(x, w1, w3, w2, g, beta, alpha):
    n, d = x.shape
    f = w1.shape[1]
    assert f % FF_CHUNK == 0
    tm = min(ROW_TILE, n)
    return pl.pallas_call(
        functools.partial(_ffn_kernel, alpha=alpha),
        grid=(n // tm,),
        in_specs=[pl.BlockSpec((tm, d), lambda i: (i, 0)),
                  _const_spec((d, f)), _const_spec((d, f)), _const_spec((f, d)),
                  _const_spec((1, d)), _const_spec((1, d))],
        out_specs=pl.BlockSpec((tm, d), lambda i: (i, 0)),
        out_shape=jax.ShapeDtypeStruct((n, d), F32),
        compiler_params=_cparams(("parallel",)),
        name="ffn_ln",
    )(x, w1, w3, w2, g[None, :], beta[None, :])


def _conv_kernel(x_ref, xp_ref, wpw1_ref, bpw1_ref, dww_ref, dwb_ref, cg_ref, cb_ref,
                 wpw2_ref, bpw2_ref, g_ref, beta_ref, o_ref, h_scr, c_scr, *, alpha, per_seq):
    i = pl.program_id(0)
    tm, ch = x_ref.shape

    def glu(xv):
        xb = xv.astype(BF16)
        a = _dot(xb, wpw1_ref[:, :ch]) + bpw1_ref[:, :ch]
        g = _dot(xb, wpw1_ref[:, ch:]) + bpw1_ref[:, ch:]
        return a * _sigmoid(g)

    halo = glu(xp_ref[...])
    h_scr[0:CONV_HALO, :] = jnp.where(i % per_seq == 0, 0.0, halo)
    x = x_ref[...]
    h_scr[CONV_HALO:CONV_HALO + tm, :] = glu(x)
    h_scr[CONV_HALO + tm:CONV_HALO + tm + CONV_ROWS, :] = jnp.zeros((CONV_ROWS, ch), F32)

    first = CONV_HALO - (CONV_WIDTH - 1)
    sub = lax.broadcasted_iota(I32, (CONV_ROWS, HEAD_W), 0)

    for lt in range(ch // HEAD_W):
        lanes = slice(lt * HEAD_W, (lt + 1) * HEAD_W)

        def partials(base):
            out = []
            for s in range(CONV_ROWS):
                acc = None
                for a in range((first + CONV_WIDTH - 1) // CONV_ROWS + 1):
                    k = CONV_ROWS * a + s - first
                    if 0 <= k < CONV_WIDTH:
                        r = pl.multiple_of(base + CONV_ROWS * a, CONV_ROWS)
                        term = h_scr[pl.ds(r, CONV_ROWS), lanes] * dww_ref[k:k + 1, lanes]
                        acc = term if acc is None else acc + term
                out.append(acc)
            return tuple(out)

        def rows(c, g_old):
            r0 = pl.multiple_of(c * CONV_ROWS, CONV_ROWS)
            g_new = partials(r0 + CONV_ROWS)
            acc = g_old[0]
            for s in range(1, CONV_ROWS):
                acc = acc + pltpu.roll(jnp.where(sub >= s, g_old[s], g_new[s]), CONV_ROWS - s, 0)
            c_scr[pl.ds(r0, CONV_ROWS), lanes] = acc
            return g_new

        lax.fori_loop(0, tm // CONV_ROWS, rows, partials(0))

    y = _layer_norm(c_scr[...] + dwb_ref[...], cg_ref[...], cb_ref[...])
    y = _dot((y * _sigmoid(y)).astype(BF16), wpw2_ref[...]) + bpw2_ref[...]
    o_ref[...] = _layer_norm(alpha * x + y, g_ref[...], beta_ref[...])


def _conv_ln(x, wpw1, bpw1, dww, dwb, cg, cb, wpw2, bpw2, g, beta, seq, alpha):
    n, d = x.shape
    tm = min(ROW_TILE, seq)
    per_seq = seq // tm
    halo_per_tile = tm // CONV_HALO
    vec = lambda w: _const_spec((1, w))
    return pl.pallas_call(
        functools.partial(_conv_kernel, alpha=alpha, per_seq=per_seq),
        grid=(n // tm,),
        in_specs=[pl.BlockSpec((tm, d), lambda i: (i, 0)),
                  pl.BlockSpec((CONV_HALO, d), lambda i: (jnp.maximum(i * halo_per_tile - 1, 0), 0)),
                  _const_spec((d, 2 * d)), vec(2 * d), _const_spec((CONV_WIDTH, d)), vec(d), vec(d), vec(d),
                  _const_spec((d, d)), vec(d), vec(d), vec(d)],
        out_specs=pl.BlockSpec((tm, d), lambda i: (i, 0)),
        out_shape=jax.ShapeDtypeStruct((n, d), F32),
        scratch_shapes=[pltpu.VMEM((tm + CONV_HALO + CONV_ROWS, d), F32), pltpu.VMEM((tm, d), F32)],
        compiler_params=_cparams(("parallel",)),
        name="conv_module_ln",
    )(x, x, wpw1, bpw1[None, :], dww, dwb[None, :], cg[None, :], cb[None, :],
      wpw2, bpw2[None, :], g[None, :], beta[None, :])


def _router_kernel(x_ref, wh_ref, wl_ref, b_ref, meta_ref, xb_ref, count_scr):
    i = pl.program_id(0)
    tm = x_ref.shape[0]
    rows = wh_ref.shape[0]

    @pl.when(i == 0)
    def _():
        count_scr[...] = jnp.zeros_like(count_scr)

    x = x_ref[...]
    x_hi = x.astype(BF16)
    x_lo = (x - x_hi.astype(F32)).astype(BF16)
    xb_ref[...] = x_hi
    logits = (_dot_nt(wh_ref[...], x_hi) + _dot_nt(wh_ref[...], x_lo) + _dot_nt(wl_ref[...], x_hi)
              + b_ref[:, 0:1])
    sub = lax.broadcasted_iota(I32, (rows, tm), 0)
    m1 = jnp.max(logits, axis=0, keepdims=True)
    i1 = jnp.min(jnp.where(logits == m1, sub, rows), axis=0, keepdims=True)
    rest = jnp.where(sub == i1, NEG_BIG * 2, logits)
    m2 = jnp.max(rest, axis=0, keepdims=True)
    i2 = jnp.min(jnp.where(rest == m2, sub, rows), axis=0, keepdims=True)
    t = jnp.exp(m2 - m1)
    g1 = 1.0 / (1.0 + t)
    g2 = t / (1.0 + t)

    pick1 = sub == i1
    pick2 = sub == i2
    chosen = jnp.where(pick1, 1.0, jnp.where(pick2, 1.0, 0.0))
    r = lax.broadcasted_iota(I32, (tm, tm), 0)
    c = lax.broadcasted_iota(I32, (tm, tm), 1)
    earlier = jnp.where(r < c, 1.0, 0.0).astype(BF16)
    before = count_scr[:, 0:1]
    rank = _dot(chosen.astype(BF16), earlier) + before
    r1 = jnp.sum(jnp.where(pick1, rank, 0.0), axis=0, keepdims=True)
    r2 = jnp.sum(jnp.where(pick2, rank, 0.0), axis=0, keepdims=True)

    out_row = lax.broadcasted_iota(I32, meta_ref.shape, 0)
    meta = jnp.where(out_row == 0, i1.astype(F32), 0.0)
    meta = jnp.where(out_row == 1, i2.astype(F32), meta)
    meta = jnp.where(out_row == 2, r1, meta)
    meta = jnp.where(out_row == 3, r2, meta)
    meta = jnp.where(out_row == 4, g1, meta)
    meta = jnp.where(out_row == 5, g2, meta)
    meta_ref[...] = meta

    total = before + jnp.sum(chosen, axis=1, keepdims=True)
    count_scr[...] = jnp.broadcast_to(total, count_scr.shape)


def _router(x, w_hi, w_lo, b_col):
    n, d = x.shape
    tm = min(ROUTER_TILE, n)
    rows = w_hi.shape[0]
    return pl.pallas_call(
        _router_kernel,
        grid=(n // tm,),
        in_specs=[pl.BlockSpec((tm, d), lambda i: (i, 0)), _const_spec((rows, d)), _const_spec((rows, d)),
                  _const_spec((rows, HEAD_W))],
        out_specs=[pl.BlockSpec((8, tm), lambda i: (0, i)),
                   pl.BlockSpec((tm, d), lambda i: (i, 0))],
        out_shape=[jax.ShapeDtypeStruct((8, n), F32),
                   jax.ShapeDtypeStruct((n, d), BF16)],
        scratch_shapes=[pltpu.VMEM((rows, HEAD_W), F32)],
        compiler_params=_cparams(("arbitrary",)),
        name="moe_router",
    )(x, w_hi, w_lo, b_col)


def _dispatch_kernel(seg_ref, x_ref, p1_ref, p2_ref, zeros_hbm, o_hbm, buf, sem):
    del zeros_hbm
    i = pl.program_id(0)
    n_steps = pl.num_programs(0)
    ts = x_ref.shape[0]
    rows = buf.shape[1]
    sizes = [SEG_ALIGN << b for b in range((ts // SEG_ALIGN).bit_length())]

    def segment_copies(step, slot, fn):
        for e in range(N_EXPERTS):
            length = seg_ref[step * 3 * N_EXPERTS + e]
            src = seg_ref[step * 3 * N_EXPERTS + N_EXPERTS + e]
            dst = seg_ref[step * 3 * N_EXPERTS + 2 * N_EXPERTS + e]
            done = 0 * length
            for size in sizes:
                copy = pltpu.make_async_copy(
                    buf.at[slot, pl.ds(pl.multiple_of(src + done, SEG_ALIGN), size), :],
                    o_hbm.at[pl.ds(pl.multiple_of(dst + done, SEG_ALIGN), size), :],
                    sem.at[slot, e])

                @pl.when((length & size) != 0)
                def _():
                    fn(copy)

                done = done + (length & size)

    slot = i % 2

    @pl.when(i >= 2)
    def _():
        segment_copies(i - 2, slot, lambda c: c.wait())

    rid = lax.broadcasted_iota(I32, (rows, ts), 0)
    onehot = jnp.where(p1_ref[...] == rid, 1.0, jnp.where(p2_ref[...] == rid, 1.0, 0.0)).astype(BF16)
    buf[slot] = _dot(onehot, x_ref[...]).astype(buf.dtype)
    segment_copies(i, slot, lambda c: c.start())

    @pl.when(i == n_steps - 1)
    def _():
        @pl.when(i >= 1)
        def _():
            segment_copies(i - 1, 1 - slot, lambda c: c.wait())

        segment_copies(i, slot, lambda c: c.wait())


def _dispatch(xb, pos1, pos2, seg, n_rows):
    n, d = xb.shape
    ts = min(SRC_TILE, n)
    n_src = n // ts
    rows = 2 * ts + N_EXPERTS * SEG_ALIGN
    pspec = pl.BlockSpec((None, 1, ts), lambda i, seg: (i, 0, 0))
    grid_spec = pltpu.PrefetchScalarGridSpec(
        num_scalar_prefetch=1,
        grid=(n_src,),
        in_specs=[pl.BlockSpec((ts, d), lambda i, seg: (i, 0)), pspec, pspec,
                  pl.BlockSpec(memory_space=pl.ANY)],
        out_specs=pl.BlockSpec(memory_space=pl.ANY),
        scratch_shapes=[pltpu.VMEM((2, rows, d), BF16), pltpu.SemaphoreType.DMA((2, N_EXPERTS))],
    )
    return pl.pallas_call(
        _dispatch_kernel,
        grid_spec=grid_spec,
        out_shape=jax.ShapeDtypeStruct((n_rows, d), BF16),
        input_output_aliases={4: 0},
        compiler_params=_cparams(("arbitrary",)),
        name="moe_dispatch",
    )(seg, xb, pos1.reshape(n_src, 1, ts), pos2.reshape(n_src, 1, ts), jnp.zeros((n_rows, d), BF16))


def _expert_kernel(be_ref, bv_ref, x_ref, w1_ref, w3_ref, w2_ref, o_ref):
    j = pl.program_id(0)

    @pl.when(bv_ref[j] > 0)
    def _():
        y = _swiglu_acc(x_ref[...], w1_ref, w3_ref, w2_ref, x_ref.shape[0], o_ref.shape[1])
        o_ref[...] = y.astype(o_ref.dtype)

    @pl.when(bv_ref[j] == 0)
    def _():
        o_ref[...] = jnp.zeros_like(o_ref)


def _experts(xs, w1, w3, w2, block_e, block_rows):
    n_rows, d = xs.shape
    f = w1.shape[2]
    assert f % FF_CHUNK == 0
    n_blocks = n_rows // EXPERT_BLOCK
    n_steps = block_e.shape[0]
    grid_spec = pltpu.PrefetchScalarGridSpec(
        num_scalar_prefetch=2,
        grid=(n_steps,),
        in_specs=[pl.BlockSpec((EXPERT_BLOCK, d), lambda j, be, bv: (jnp.minimum(j, n_blocks - 1), 0)),
                  pl.BlockSpec((None, d, f), lambda j, be, bv: (be[j], 0, 0)),
                  pl.BlockSpec((None, d, f), lambda j, be, bv: (be[j], 0, 0)),
                  pl.BlockSpec((None, f, d), lambda j, be, bv: (be[j], 0, 0))],
        out_specs=pl.BlockSpec((EXPERT_BLOCK, d), lambda j, be, bv: (j, 0)),
    )
    return pl.pallas_call(
        _expert_kernel,
        grid_spec=grid_spec,
        out_shape=jax.ShapeDtypeStruct((n_steps * EXPERT_BLOCK, d), BF16),
        compiler_params=_cparams(("arbitrary",)),
        name="moe_experts",
    )(block_e, block_rows, xs, w1, w3, w2)


def _combine_kernel(wb_ref, x_ref, cm_ref, y_hbm, g_ref, beta_ref, o_ref, win, sem, *, alpha):
    i = pl.program_id(0)
    n_steps = pl.num_programs(0)
    tm = x_ref.shape[0]

    def window_copy(step, slot, e):
        start = pl.multiple_of(wb_ref[step * N_EXPERTS + e], COMB_ALIGN)
        return pltpu.make_async_copy(y_hbm.at[pl.ds(start, COMB_WIN), :],
                                     win.at[slot, pl.ds(e * COMB_WIN, COMB_WIN), :],
                                     sem.at[slot, e])

    @pl.when(i == 0)
    def _():
        for e in range(N_EXPERTS):
            window_copy(0, 0, e).start()

    slot = i % 2

    @pl.when(i + 1 < n_steps)
    def _():
        for e in range(N_EXPERTS):
            window_copy(i + 1, 1 - slot, e).start()

    for e in range(N_EXPERTS):
        window_copy(i, slot, e).wait()

    cm = cm_ref[...]
    col = lax.broadcasted_iota(I32, (tm, N_EXPERTS * COMB_WIN), 1).astype(F32)
    p1 = jnp.where(col == cm[:, 0:1], 1.0, 0.0).astype(BF16)
    p2 = jnp.where(col == cm[:, 1:2], 1.0, 0.0).astype(BF16)
    rows = _dot(jnp.concatenate([p1, p2], axis=0), win[slot])
    y = cm[:, 2:3] * rows[:tm] + cm[:, 3:4] * rows[tm:]
    o_ref[...] = _layer_norm(alpha * x_ref[...] + y, g_ref[...], beta_ref[...])


def _combine_ln(x, cm, wbase, yb, g, beta, alpha):
    n, d = x.shape
    tm = min(COMB_TILE, n)
    grid_spec = pltpu.PrefetchScalarGridSpec(
        num_scalar_prefetch=1,
        grid=(n // tm,),
        in_specs=[pl.BlockSpec((tm, d), lambda i, wb: (i, 0)),
                  pl.BlockSpec((tm, 8), lambda i, wb: (i, 0)),
                  pl.BlockSpec(memory_space=pl.ANY),
                  pl.BlockSpec((1, d), lambda i, wb: (0, 0)),
                  pl.BlockSpec((1, d), lambda i, wb: (0, 0))],
        out_specs=pl.BlockSpec((tm, d), lambda i, wb: (i, 0)),
        scratch_shapes=[pltpu.VMEM((2, N_EXPERTS * COMB_WIN, d), BF16),
                        pltpu.SemaphoreType.DMA((2, N_EXPERTS))],
    )
    return pl.pallas_call(
        functools.partial(_combine_kernel, alpha=alpha),
        grid_spec=grid_spec,
        out_shape=jax.ShapeDtypeStruct((n, d), F32),
        compiler_params=_cparams(("arbitrary",)),
        name="moe_combine_ln",
    )(wbase, x, cm, yb, g[None, :], beta[None, :])


def _moe_ln(x, router_w, router_b, w1, w3, w2, g, beta, alpha):
    n, d = x.shape
    n_exp = router_w.shape[1]
    assert n_exp == N_EXPERTS
    pad = ROUTER_ROWS - n_exp
    w_t = jnp.pad(router_w.T, ((0, pad), (0, 0)))
    w_hi = w_t.astype(BF16)
    w_lo = (w_t - w_hi.astype(F32)).astype(BF16)
    b_col = jnp.broadcast_to(jnp.pad(router_b, (0, pad), constant_values=NEG_BIG)[:, None], (ROUTER_ROWS, HEAD_W))
    meta, xb = _router(x, w_hi, w_lo, b_col)

    e1, e2 = meta[0].astype(I32), meta[1].astype(I32)
    r1, r2 = meta[2].astype(I32), meta[3].astype(I32)
    experts = jnp.arange(n_exp, dtype=I32)[None, :]
    pick1 = (e1[:, None] == experts).astype(I32)
    pick2 = (e2[:, None] == experts).astype(I32)

    ct = min(COMB_TILE, n)
    ts = min(SRC_TILE, n)
    n_src = n // ts
    per_ct = jnp.sum((pick1 + pick2).reshape(n // ct, ct, n_exp), axis=1)
    before_ct = jnp.cumsum(per_ct, axis=0) - per_ct
    per_src = jnp.sum(per_ct.reshape(n_src, ts // ct, n_exp), axis=1)
    before_src = before_ct[:: ts // ct]
    seg_len = (per_src + SEG_ALIGN - 1) // SEG_ALIGN * SEG_ALIGN
    local_off = jnp.cumsum(seg_len, axis=1) - seg_len
    rows_e = jnp.sum(seg_len, axis=0)
    region = (rows_e + EXPERT_BLOCK - 1) // EXPERT_BLOCK * EXPERT_BLOCK
    region_end = jnp.cumsum(region)
    region_start = region_end - region
    seg_start = region_start[None, :] + jnp.cumsum(seg_len, axis=0) - seg_len
    n_blocks = (n * TOP_K + n_src * n_exp * (SEG_ALIGN - 1)) // EXPERT_BLOCK + 1 + n_exp
    n_rows = n_blocks * EXPERT_BLOCK

    def per_token(table, pick):
        tiles = table.shape[0]
        tok = jnp.broadcast_to(table[:, None, :], (tiles, n // tiles, n_exp)).reshape(n, n_exp)
        return jnp.sum(pick * tok, axis=1)

    in_tile1 = r1 - per_token(before_src, pick1)
    in_tile2 = r2 - per_token(before_src, pick2)
    pos1 = per_token(local_off, pick1) + in_tile1
    pos2 = per_token(local_off, pick2) + in_tile2
    dest1 = per_token(seg_start, pick1) + in_tile1
    dest2 = per_token(seg_start, pick2) + in_tile2
    seg = jnp.concatenate([seg_len, local_off, seg_start], axis=1).reshape(-1).astype(I32)
    xs = _dispatch(xb, pos1, pos2, seg, n_rows)

    bstart = jnp.arange(n_blocks + 1, dtype=I32) * EXPERT_BLOCK
    block_e = jnp.minimum(jnp.sum((bstart[:, None] >= region_end[None, :]).astype(I32), axis=1), n_exp - 1)
    block_rows = jnp.clip((region_start + rows_e)[block_e] - bstart, 0, EXPERT_BLOCK).astype(I32)
    yb = _experts(xs, w1, w3, w2, block_e, block_rows)

    first_ct = jnp.repeat(seg_start - before_src, ts // ct, axis=0) + before_ct
    wbase = first_ct // COMB_ALIGN * COMB_ALIGN
    c1 = e1 * COMB_WIN + dest1 - per_token(wbase, pick1)
    c2 = e2 * COMB_WIN + dest2 - per_token(wbase, pick2)
    cm = jnp.stack([c1.astype(F32), c2.astype(F32), meta[4], meta[5]]
                   + [jnp.zeros((n,), F32)] * 4, axis=1)
    return _combine_ln(x, cm, wbase.reshape(-1).astype(I32), yb, g, beta, alpha)


def kernel(x, mem, ln1_g, ln1_b, lnx_g, lnx_b, ln2_g, ln2_b, xa_wq, xa_wkv, xa_wo, attn_w_in, attn_w_o, diff_lq1, diff_lk1, diff_lq2, diff_lk2, diff_subln_g, ffn_w1, ffn_w3, ffn_w2, conv_w_pw1, conv_b_pw1, conv_dw_w, conv_dw_b, conv_ln_g, conv_ln_b, conv_w_pw2, conv_b_pw2, moe_router_w, moe_router_b, moe_w1, moe_w3, moe_w2):
    batch, seq, d = x.shape
    n_mem = mem.shape[1]
    depth = ln1_g.shape[0]
    alpha = (2 * depth) ** 0.25
    sb_width = SB_HEADS * HEAD_W

    bf = lambda w: w.astype(BF16)
    cos_t, sa_t, sb_t = _rope_tables(seq)
    h = x.reshape(batch * seq, d)
    mem2 = mem.reshape(batch * n_mem, d)

    for i in range(depth):
        j = i // 2
        if i % 2 == 0:
            lambda_init = 0.8 - 0.6 * math.exp(-0.3 * i)
            qkv = _inproj(h, bf(attn_w_in[j]), cos_t, sa_t, sb_t, seq)
            o_sb = _sb_attention(qkv, batch, seq)
            o_df = _diff_attention(qkv, diff_lq1[j], diff_lk1[j], diff_lq2[j], diff_lk2[j],
                                   diff_subln_g[j], batch, seq, lambda_init)
            w_o = bf(attn_w_o[j])
            h = _proj2_ln(o_sb, o_df, w_o[:sb_width], w_o[sb_width:], h, ln1_g[i], ln1_b[i], alpha)
        else:
            h = _conv_ln(h, bf(conv_w_pw1[j]), conv_b_pw1[j], conv_dw_w[j], conv_dw_b[j],
                         conv_ln_g[j], conv_ln_b[j], bf(conv_w_pw2[j]), conv_b_pw2[j],
                         ln1_g[i], ln1_b[i], seq, alpha)
        kv = _matmul(mem2, bf(xa_wkv[i]), "xattn_kv")
        h = _xattn_ln(h, kv, bf(xa_wq[i]), bf(xa_wo[i]), lnx_g[i], lnx_b[i], seq, n_mem, alpha)
        if i % 2 == 0:
            h = _ffn_ln(h, bf(ffn_w1[j]), bf(ffn_w3[j]), bf(ffn_w2[j]), ln2_g[i], ln2_b[i], alpha)
        else:
            h = _moe_ln(h, moe_router_w[j], moe_router_b[j], bf(moe_w1[j]), bf(moe_w3[j]), bf(moe_w2[j]),
                        ln2_g[i], ln2_b[i], alpha)
    return h.reshape(batch, seq, d)
```

```python
import functools
import math

import jax
import jax.numpy as jnp
from jax import lax
from jax.experimental import pallas as pl
from jax.experimental.pallas import tpu as pltpu

F32 = jnp.float32
BF16 = jnp.bfloat16
I32 = jnp.int32

LN_EPS = 1e-5
NEG_BIG = -1e30

CHUNK = 64
SB_HEADS = 4
HEAD_W = 128
DIFF_HEADS = 4
DIFF_HEAD_DIM = 64
ROPE_THETA = 10000.0
CONV_WIDTH = 31
CONV_HALO = 32
XA_HEADS = 4
N_EXPERTS = 8
TOP_K = 2

VMEM_LIMIT = 56 * 1024 * 1024
ROW_TILE = 512
SB_TQ = 1024
SB_TK = 256
SB_UNROLL = 4
DIFF_TQ = 512
DIFF_TK = 512
LOG2E = 1.4426950408889634
FF_CHUNK = 256
CONV_ROWS = 8
ROUTER_TILE = 512
ROUTER_ROWS = 16
SRC_TILE = 512
SEG_ALIGN = 16
EXPERT_BLOCK = 512
COMB_TILE = 128
COMB_ALIGN = 16
XA_TILE = 1024


def _cparams(sem):
    return pltpu.CompilerParams(dimension_semantics=sem, vmem_limit_bytes=VMEM_LIMIT)


def _layer_norm(y, g, b):
    mu = jnp.mean(y, axis=-1, keepdims=True)
    d = y - mu
    var = jnp.mean(d * d, axis=-1, keepdims=True)
    return d * lax.rsqrt(var + LN_EPS) * g + b


def _sigmoid(x):
    return 1.0 / (1.0 + jnp.exp(-x))


def _neg_abs(x):
    bits = lax.bitcast_convert_type(x, jnp.uint32) | jnp.uint32(0x80000000)
    return lax.bitcast_convert_type(bits, F32)


def _dot(a, b):
    return jnp.dot(a, b, preferred_element_type=F32)


def _dot_nt(a, b):
    return lax.dot_general(a, b, (((1,), (1,)), ((), ())), preferred_element_type=F32)


def _const_spec(shape):
    nd = len(shape)
    return pl.BlockSpec(shape, lambda *_: (0,) * nd)


def _matmul_kernel(x_ref, w_ref, o_ref):
    o_ref[...] = _dot(x_ref[...].astype(BF16), w_ref[...]).astype(o_ref.dtype)


def _matmul(x, w, name):
    n, d = x.shape
    m = w.shape[1]
    tm = min(ROW_TILE, n)
    return pl.pallas_call(
        _matmul_kernel,
        grid=(n // tm,),
        in_specs=[pl.BlockSpec((tm, d), lambda i: (i, 0)), _const_spec((d, m))],
        out_specs=pl.BlockSpec((tm, m), lambda i: (i, 0)),
        out_shape=jax.ShapeDtypeStruct((n, m), BF16),
        compiler_params=_cparams(("parallel",)),
        name=name,
    )(x, w)


def _inproj_kernel(x_ref, w_ref, cos_ref, sa_ref, sb_ref, o_ref, *, width, sq_scale, dq_scale):
    xb = x_ref[...].astype(BF16)
    n_groups = o_ref.shape[1] // width
    for j in range(n_groups):
        acc = _dot(xb, w_ref[:, j * width:(j + 1) * width])
        if j in (3, 4):
            parts = []
            for c in range(width // HEAD_W):
                hc = acc[:, c * HEAD_W:(c + 1) * HEAD_W]
                rot = (hc * cos_ref[...]
                       + pltpu.roll(hc, HEAD_W - DIFF_HEAD_DIM // 2, 1) * sa_ref[...]
                       + pltpu.roll(hc, DIFF_HEAD_DIM // 2, 1) * sb_ref[...])
                parts.append(rot)
            acc = jnp.concatenate(parts, axis=1)
            if j == 3:
                acc = acc * dq_scale
        if j == 0:
            acc = acc * sq_scale
        o_ref[:, j * width:(j + 1) * width] = acc.astype(o_ref.dtype)


def _inproj(x, w, cos_t, sa_t, sb_t, seq):
    n, d = x.shape
    m = w.shape[1]
    tm = min(ROW_TILE, seq)
    per_seq = seq // tm
    tab = pl.BlockSpec((tm, HEAD_W), lambda i: (i % per_seq, 0))
    return pl.pallas_call(
        functools.partial(_inproj_kernel, width=m // 6, sq_scale=HEAD_W ** -0.5 * LOG2E,
                          dq_scale=DIFF_HEAD_DIM ** -0.5 * LOG2E),
        grid=(n // tm,),
        in_specs=[pl.BlockSpec((tm, d), lambda i: (i, 0)), _const_spec((d, m)), tab, tab, tab],
        out_specs=pl.BlockSpec((tm, m), lambda i: (i, 0)),
        out_shape=jax.ShapeDtypeStruct((n, m), BF16),
        compiler_params=_cparams(("parallel",)),
        name="attn_inproj",
    )(x, w, cos_t, sa_t, sb_t)


def _rope_tables(seq):
    half = DIFF_HEAD_DIM // 2
    inv = ROPE_THETA ** (-jnp.arange(0, DIFF_HEAD_DIM, 2, dtype=F32) / DIFF_HEAD_DIM)
    ang = jnp.arange(seq, dtype=F32)[:, None] * inv[None, :]
    cos, sin = jnp.cos(ang), jnp.sin(ang)
    zero = jnp.zeros_like(sin)
    reps = HEAD_W // DIFF_HEAD_DIM
    cos_t = jnp.tile(jnp.concatenate([cos, cos], axis=1), (1, reps))
    sa_t = jnp.tile(jnp.concatenate([-sin, zero], axis=1), (1, reps))
    sb_t = jnp.tile(jnp.concatenate([zero, sin], axis=1), (1, reps))
    del half
    return cos_t, sa_t, sb_t


def _sb_kernel(q_ref, k_ref, v_ref, o_ref, acc_ref, carry_ref, za_ref, zb_ref, *, tq, tk):
    i = pl.program_id(2)
    band = tq // tk
    row = lax.broadcasted_iota(I32, (tk, tk), 0)
    col = lax.broadcasted_iota(I32, (tk, tk), 1)
    suffix = jnp.where(row >= col, 1.0, 0.0).astype(BF16)

    def logits(kb):
        k0 = pl.multiple_of(kb * tk, tk)
        return _dot_nt(q_ref[...], k_ref[pl.ds(k0, tk), :])

    def rest(z, kb, rows, strict=None):
        k0 = pl.multiple_of(kb * tk, tk)
        drop = jnp.maximum(z, 0.0) + jnp.log(1.0 + jnp.exp2(_neg_abs(z))) * LOG2E
        if strict is not None:
            drop = jnp.where(strict, drop, 0.0)
        hi = drop.astype(BF16)
        lo = (drop - hi.astype(F32)).astype(BF16)
        within = _dot(hi, suffix) + _dot(lo, suffix)
        w = jnp.exp2(z - within - carry_ref[rows, :])
        if strict is not None:
            w = jnp.where(strict, w, 0.0)
        acc_ref[rows, :] += _dot(w.astype(BF16), v_ref[pl.ds(k0, tk), :])
        carry_ref[rows, :] += within[:, 0:1]

    acc_ref[...] = jnp.zeros_like(acc_ref)
    carry_ref[...] = jnp.zeros_like(carry_ref)
    strict = col < row
    for c in reversed(range(band)):
        kb = i * band + c
        k0 = pl.multiple_of(kb * tk, tk)
        diag = slice(c * tk, (c + 1) * tk)
        rest(_dot_nt(q_ref[diag, :], k_ref[pl.ds(k0, tk), :]), kb, diag, strict)
        if c + 1 < band:
            lower = slice((c + 1) * tk, tq)
            rest(_dot_nt(q_ref[lower, :], k_ref[pl.ds(k0, tk), :]), kb, lower)
    every = slice(0, tq)
    below = i * band

    zs = (za_ref, zb_ref)
    rem = below % SB_UNROLL

    def single(t, c):
        kb = below - 1 - t
        rest(logits(kb), kb, every)
        return c

    lax.fori_loop(0, rem, single, 0)
    top = below - rem
    za_ref[...] = logits(jnp.maximum(top - 1, 0))

    def body(t, c):
        kb = top - 1 - SB_UNROLL * t
        for u in range(SB_UNROLL):
            zs[(u + 1) % 2][...] = logits(jnp.maximum(kb - u - 1, 0))
            rest(zs[u % 2][...], kb - u, every)
        return c

    lax.fori_loop(0, top // SB_UNROLL, body, 0)
    o_ref[...] = acc_ref[...].astype(o_ref.dtype)


def _whole_seq_spec(seq, first):
    return pl.BlockSpec((seq, HEAD_W), lambda b, h, i: (b, first + h), pipeline_mode=pl.Buffered(1))


def _sb_attention(qkv, batch, seq):
    tq = min(SB_TQ, seq)
    tk = min(SB_TK, seq)
    assert SB_UNROLL % 2 == 0 and tq % tk == 0
    nq = seq // tq
    return pl.pallas_call(
        functools.partial(_sb_kernel, tq=tq, tk=tk),
        grid=(batch, SB_HEADS, nq),
        in_specs=[
            pl.BlockSpec((tq, HEAD_W), lambda b, h, i: (b * nq + i, h)),
            _whole_seq_spec(seq, SB_HEADS), _whole_seq_spec(seq, 2 * SB_HEADS),
        ],
        out_specs=pl.BlockSpec((tq, HEAD_W), lambda b, h, i: (b * nq + i, h)),
        out_shape=jax.ShapeDtypeStruct((batch * seq, SB_HEADS * HEAD_W), BF16),
        scratch_shapes=[pltpu.VMEM((tq, HEAD_W), F32), pltpu.VMEM((tq, 1), F32),
                        pltpu.VMEM((tq, tk), F32), pltpu.VMEM((tq, tk), F32)],
        compiler_params=_cparams(("parallel", "parallel", "arbitrary")),
        name="stickbreak_attn",
    )(qkv, qkv, qkv)


def _diff_kernel(lq1_ref, lk1_ref, lq2_ref, lk2_ref, g_ref, q_ref, k_ref, v_ref, o_ref,
                 qq_ref, m_ref, l_ref, acc_ref, za_ref, zb_ref, *, tq, tk, lambda_init):
    i = pl.program_id(2)
    q = q_ref[...]
    lane = lax.broadcasted_iota(I32, (tq, HEAD_W), 1)
    zero = jnp.zeros_like(q)
    qq_ref[0:tq, :] = jnp.where(lane < DIFF_HEAD_DIM, q, zero)
    qq_ref[tq:2 * tq, :] = jnp.where(lane >= DIFF_HEAD_DIM, q, zero)
    m_ref[...] = jnp.full(m_ref.shape, NEG_BIG, F32)
    l_ref[...] = jnp.zeros_like(l_ref)
    acc_ref[...] = jnp.zeros_like(acc_ref)

    def logits(kb):
        k0 = pl.multiple_of(kb * tk, tk)
        return _dot_nt(qq_ref[...], k_ref[pl.ds(k0, tk), :])

    ones = jnp.ones((tk, HEAD_W), BF16)
    reps = tk // HEAD_W

    def update(z, kb):
        k0 = pl.multiple_of(kb * tk, tk)
        m_old = m_ref[...]
        m_new = jnp.maximum(m_old, jnp.max(z, axis=1, keepdims=True))
        p = jnp.exp2(z - jnp.concatenate([m_new] * reps, axis=1))
        a = jnp.exp2(m_old - m_new)
        pv = _dot(p.astype(BF16), jnp.concatenate([v_ref[pl.ds(k0, tk), :], ones], axis=1))
        l_ref[...] = a * l_ref[...] + pv[:, HEAD_W:]
        acc_ref[...] = a * acc_ref[...] + pv[:, :HEAD_W]
        m_ref[...] = m_new

    row = lax.broadcasted_iota(I32, (2 * tq, tk), 0)
    col = lax.broadcasted_iota(I32, (2 * tq, tk), 1)
    q_pos = jnp.where(row >= tq, row - tq, row)
    allowed = (col // CHUNK) <= (q_pos // CHUNK)
    za_ref[...] = jnp.where(allowed, logits(i), NEG_BIG)

    def body(t, c):
        kb = i - 2 * t
        zb_ref[...] = logits(kb - 1)
        update(za_ref[...], kb)
        za_ref[...] = logits(jnp.maximum(kb - 2, 0))
        update(zb_ref[...], kb - 1)
        return c

    lax.fori_loop(0, (i + 1) // 2, body, 0)

    @pl.when(i % 2 == 0)
    def _():
        update(za_ref[...], 0)

    lam = (jnp.exp(jnp.sum(lq1_ref[...] * lk1_ref[...], axis=1, keepdims=True))
           - jnp.exp(jnp.sum(lq2_ref[...] * lk2_ref[...], axis=1, keepdims=True)) + lambda_init)
    o = acc_ref[0:tq, :] / l_ref[0:tq, :] - lam * (acc_ref[tq:2 * tq, :] / l_ref[tq:2 * tq, :])
    rms = lax.rsqrt(jnp.mean(o * o, axis=-1, keepdims=True) + LN_EPS)
    o_ref[...] = (o * rms * g_ref[...] * (1.0 - lambda_init)).astype(o_ref.dtype)


def _diff_attention(qkv, lq1, lk1, lq2, lk2, subln_g, batch, seq, lambda_init):
    tq = min(DIFF_TQ, seq)
    tk = min(DIFF_TK, seq)
    assert tq == tk and tk % CHUNK == 0
    nq = seq // tq
    base = 3 * SB_HEADS
    vec = _const_spec((1, DIFF_HEAD_DIM))
    return pl.pallas_call(
        functools.partial(_diff_kernel, tq=tq, tk=tk, lambda_init=lambda_init),
        grid=(batch, DIFF_HEADS, nq),
        in_specs=[
            vec, vec, vec, vec, _const_spec((1, HEAD_W)),
            pl.BlockSpec((tq, HEAD_W), lambda b, h, i: (b * nq + i, base + h)),
            _whole_seq_spec(seq, base + DIFF_HEADS), _whole_seq_spec(seq, base + 2 * DIFF_HEADS),
        ],
        out_specs=pl.BlockSpec((tq, HEAD_W), lambda b, h, i: (b * nq + i, h)),
        out_shape=jax.ShapeDtypeStruct((batch * seq, DIFF_HEADS * HEAD_W), BF16),
        scratch_shapes=[pltpu.VMEM((2 * tq, HEAD_W), BF16), pltpu.VMEM((2 * tq, HEAD_W), F32),
                        pltpu.VMEM((2 * tq, HEAD_W), F32), pltpu.VMEM((2 * tq, HEAD_W), F32),
                        pltpu.VMEM((2 * tq, tk), F32), pltpu.VMEM((2 * tq, tk), F32)],
        compiler_params=_cparams(("parallel", "parallel", "arbitrary")),
        name="diff_attn",
    )(lq1[None, :], lk1[None, :], lq2[None, :], lk2[None, :], subln_g[None, :], qkv, qkv, qkv)


def _proj2_ln_kernel(a_ref, b_ref, wa_ref, wb_ref, x_ref, g_ref, beta_ref, o_ref, *, alpha):
    y = _dot(a_ref[...], wa_ref[...]) + _dot(b_ref[...], wb_ref[...])
    o_ref[...] = _layer_norm(alpha * x_ref[...] + y, g_ref[...], beta_ref[...])


def _proj2_ln(a, b, wa, wb, x, g, beta, alpha):
    n, d = x.shape
    tm = min(ROW_TILE, n)
    ka, kb = a.shape[1], b.shape[1]
    row = lambda w: pl.BlockSpec((tm, w), lambda i: (i, 0))
    return pl.pallas_call(
        functools.partial(_proj2_ln_kernel, alpha=alpha),
        grid=(n // tm,),
        in_specs=[row(ka), row(kb), _const_spec((ka, d)), _const_spec((kb, d)), row(d),
                  _const_spec((1, d)), _const_spec((1, d))],
        out_specs=row(d),
        out_shape=jax.ShapeDtypeStruct((n, d), F32),
        compiler_params=_cparams(("parallel",)),
        name="attn_outproj_ln",
    )(a, b, wa, wb, x, g[None, :], beta[None, :])


def _xattn_kernel(x_ref, kv_ref, wq_ref, wo_ref, g_ref, beta_ref, o_ref, *, alpha, scale):
    x = x_ref[...]
    q = _dot(x.astype(BF16), wq_ref[...]).astype(BF16)
    width = wq_ref.shape[1]
    outs = []
    for h in range(XA_HEADS):
        qh = q[:, h * HEAD_W:(h + 1) * HEAD_W]
        kh = kv_ref[:, h * HEAD_W:(h + 1) * HEAD_W]
        vh = kv_ref[:, width + h * HEAD_W:width + (h + 1) * HEAD_W]
        z = _dot_nt(qh, kh) * scale
        z = z - jnp.max(z, axis=1, keepdims=True)
        p = jnp.exp(z)
        p = p / jnp.sum(p, axis=1, keepdims=True)
        outs.append(_dot(p.astype(BF16), vh))
    o = jnp.concatenate(outs, axis=1).astype(BF16)
    y = _dot(o, wo_ref[...])
    o_ref[...] = _layer_norm(alpha * x + y, g_ref[...], beta_ref[...])


def _xattn_ln(x, kv, wq, wo, g, beta, seq, n_mem, alpha):
    n, d = x.shape
    tm = min(XA_TILE, seq)
    per_seq = seq // tm
    width = wq.shape[1]
    return pl.pallas_call(
        functools.partial(_xattn_kernel, alpha=alpha, scale=HEAD_W ** -0.5),
        grid=(n // tm,),
        in_specs=[pl.BlockSpec((tm, d), lambda i: (i, 0)),
                  pl.BlockSpec((n_mem, 2 * width), lambda i: (i // per_seq, 0)),
                  _const_spec((d, width)), _const_spec((width, d)),
                  _const_spec((1, d)), _const_spec((1, d))],
        out_specs=pl.BlockSpec((tm, d), lambda i: (i, 0)),
        out_shape=jax.ShapeDtypeStruct((n, d), F32),
        compiler_params=_cparams(("parallel",)),
        name="xattn_ln",
    )(x, kv, wq, wo, g[None, :], beta[None, :])


def _swiglu_acc(xb, w1_ref, w3_ref, w2_ref, rows, d_out):
    d_ff = w1_ref.shape[1]
    acc = jnp.zeros((rows, d_out), F32)
    for c in range(d_ff // FF_CHUNK):
        sl = slice(c * FF_CHUNK, (c + 1) * FF_CHUNK)
        a = _dot(xb, w1_ref[:, sl])
        b = _dot(xb, w3_ref[:, sl])
        h = (a * _sigmoid(a) * b).astype(BF16)
        acc = acc + _dot(h, w2_ref[sl, :])
    return acc


def _ffn_kernel(x_ref, w1_ref, w3_ref, w2_ref, g_ref, beta_ref, o_ref, *, alpha):
    x = x_ref[...]
    y = _swiglu_acc(x.astype(BF16), w1_ref, w3_ref, w2_ref, x.shape[0], x.shape[1])
    o_ref[...] = _layer_norm(alpha * x + y, g_ref[...], beta_ref[...])


def _ffn_ln(x, w1, w3, w2, g, beta, alpha):
    n, d = x.shape
    f = w1.shape[1]
    assert f % FF_CHUNK == 0
    tm = min(ROW_TILE, n)
    return pl.pallas_call(
        functools.partial(_ffn_kernel, alpha=alpha),
        grid=(n // tm,),
        in_specs=[pl.BlockSpec((tm, d), lambda i: (i, 0)),
                  _const_spec((d, f)), _const_spec((d, f)), _const_spec((f, d)),
                  _const_spec((1, d)), _const_spec((1, d))],
        out_specs=pl.BlockSpec((tm, d), lambda i: (i, 0)),
        out_shape=jax.ShapeDtypeStruct((n, d), F32),
        compiler_params=_cparams(("parallel",)),
        name="ffn_ln",
    )(x, w1, w3, w2, g[None, :], beta[None, :])


def _conv_kernel(x_ref, xp_ref, wpw1_ref, bpw1_ref, dww_ref, dwb_ref, cg_ref, cb_ref,
                 wpw2_ref, bpw2_ref, g_ref, beta_ref, o_ref, h_scr, c_scr, *, alpha, per_seq):
    i = pl.program_id(0)
    tm, ch = x_ref.shape

    def glu(xv):
        xb = xv.astype(BF16)
        a = _dot(xb, wpw1_ref[:, :ch]) + bpw1_ref[:, :ch]
        g = _dot(xb, wpw1_ref[:, ch:]) + bpw1_ref[:, ch:]
        return a * _sigmoid(g)

    halo = glu(xp_ref[...])
    h_scr[0:CONV_HALO, :] = jnp.where(i % per_seq == 0, 0.0, halo)
    x = x_ref[...]
    h_scr[CONV_HALO:CONV_HALO + tm, :] = glu(x)
    h_scr[CONV_HALO + tm:CONV_HALO + tm + CONV_ROWS, :] = jnp.zeros((CONV_ROWS, ch), F32)

    first = CONV_HALO - (CONV_WIDTH - 1)
    sub = lax.broadcasted_iota(I32, (CONV_ROWS, HEAD_W), 0)

    for lt in range(ch // HEAD_W):
        lanes = slice(lt * HEAD_W, (lt + 1) * HEAD_W)

        def partials(base):
            out = []
            for s in range(CONV_ROWS):
                acc = None
                for a in range((first + CONV_WIDTH - 1) // CONV_ROWS + 1):
                    k = CONV_ROWS * a + s - first
                    if 0 <= k < CONV_WIDTH:
                        r = pl.multiple_of(base + CONV_ROWS * a, CONV_ROWS)
                        term = h_scr[pl.ds(r, CONV_ROWS), lanes] * dww_ref[k:k + 1, lanes]
                        acc = term if acc is None else acc + term
                out.append(acc)
            return tuple(out)

        def rows(c, g_old):
            r0 = pl.multiple_of(c * CONV_ROWS, CONV_ROWS)
            g_new = partials(r0 + CONV_ROWS)
            acc = g_old[0]
            for s in range(1, CONV_ROWS):
                acc = acc + pltpu.roll(jnp.where(sub >= s, g_old[s], g_new[s]), CONV_ROWS - s, 0)
            c_scr[pl.ds(r0, CONV_ROWS), lanes] = acc
            return g_new

        lax.fori_loop(0, tm // CONV_ROWS, rows, partials(0))

    y = _layer_norm(c_scr[...] + dwb_ref[...], cg_ref[...], cb_ref[...])
    y = _dot((y * _sigmoid(y)).astype(BF16), wpw2_ref[...]) + bpw2_ref[...]
    o_ref[...] = _layer_norm(alpha * x + y, g_ref[...], beta_ref[...])


def _conv_ln(x, wpw1, bpw1, dww, dwb, cg, cb, wpw2, bpw2, g, beta, seq, alpha):
    n, d = x.shape
    tm = min(ROW_TILE, seq)
    per_seq = seq // tm
    halo_per_tile = tm // CONV_HALO
    vec = lambda w: _const_spec((1, w))
    return pl.pallas_call(
        functools.partial(_conv_kernel, alpha=alpha, per_seq=per_seq),
        grid=(n // tm,),
        in_specs=[pl.BlockSpec((tm, d), lambda i: (i, 0)),
                  pl.BlockSpec((CONV_HALO, d), lambda i: (jnp.maximum(i * halo_per_tile - 1, 0), 0)),
                  _const_spec((d, 2 * d)), vec(2 * d), _const_spec((CONV_WIDTH, d)), vec(d), vec(d), vec(d),
                  _const_spec((d, d)), vec(d), vec(d), vec(d)],
        out_specs=pl.BlockSpec((tm, d), lambda i: (i, 0)),
        out_shape=jax.ShapeDtypeStruct((n, d), F32),
        scratch_shapes=[pltpu.VMEM((tm + CONV_HALO + CONV_ROWS, d), F32), pltpu.VMEM((tm, d), F32)],
        compiler_params=_cparams(("parallel",)),
        name="conv_module_ln",
    )(x, x, wpw1, bpw1[None, :], dww, dwb[None, :], cg[None, :], cb[None, :],
      wpw2, bpw2[None, :], g[None, :], beta[None, :])


def _router_kernel(x_ref, wh_ref, wl_ref, b_ref, meta_ref, xb_ref, count_scr):
    i = pl.program_id(0)
    tm = x_ref.shape[0]
    rows = wh_ref.shape[0]

    @pl.when(i == 0)
    def _():
        count_scr[...] = jnp.zeros_like(count_scr)

    x = x_ref[...]
    x_hi = x.astype(BF16)
    x_lo = (x - x_hi.astype(F32)).astype(BF16)
    xb_ref[...] = x_hi
    logits = (_dot_nt(wh_ref[...], x_hi) + _dot_nt(wh_ref[...], x_lo) + _dot_nt(wl_ref[...], x_hi)
              + b_ref[:, 0:1])
    sub = lax.broadcasted_iota(I32, (rows, tm), 0)
    m1 = jnp.max(logits, axis=0, keepdims=True)
    i1 = jnp.min(jnp.where(logits == m1, sub, rows), axis=0, keepdims=True)
    rest = jnp.where(sub == i1, NEG_BIG * 2, logits)
    m2 = jnp.max(rest, axis=0, keepdims=True)
    i2 = jnp.min(jnp.where(rest == m2, sub, rows), axis=0, keepdims=True)
    t = jnp.exp(m2 - m1)
    g1 = 1.0 / (1.0 + t)
    g2 = t / (1.0 + t)

    pick1 = sub == i1
    pick2 = sub == i2
    chosen = jnp.where(pick1, 1.0, jnp.where(pick2, 1.0, 0.0))
    r = lax.broadcasted_iota(I32, (tm, tm), 0)
    c = lax.broadcasted_iota(I32, (tm, tm), 1)
    earlier = jnp.where(r < c, 1.0, 0.0).astype(BF16)
    before = count_scr[:, 0:1]
    rank = _dot(chosen.astype(BF16), earlier) + before
    r1 = jnp.sum(jnp.where(pick1, rank, 0.0), axis=0, keepdims=True)
    r2 = jnp.sum(jnp.where(pick2, rank, 0.0), axis=0, keepdims=True)

    out_row = lax.broadcasted_iota(I32, meta_ref.shape, 0)
    meta = jnp.where(out_row == 0, i1.astype(F32), 0.0)
    meta = jnp.where(out_row == 1, i2.astype(F32), meta)
    meta = jnp.where(out_row == 2, r1, meta)
    meta = jnp.where(out_row == 3, r2, meta)
    meta = jnp.where(out_row == 4, g1, meta)
    meta = jnp.where(out_row == 5, g2, meta)
    meta_ref[...] = meta

    total = before + jnp.sum(chosen, axis=1, keepdims=True)
    count_scr[...] = jnp.broadcast_to(total, count_scr.shape)


def _router(x, w_hi, w_lo, b_col):
    n, d = x.shape
    tm = min(ROUTER_TILE, n)
    rows = w_hi.shape[0]
    return pl.pallas_call(
        _router_kernel,
        grid=(n // tm,),
        in_specs=[pl.BlockSpec((tm, d), lambda i: (i, 0)), _const_spec((rows, d)), _const_spec((rows, d)),
                  _const_spec((rows, HEAD_W))],
        out_specs=[pl.BlockSpec((8, tm), lambda i: (0, i)),
                   pl.BlockSpec((tm, d), lambda i: (i, 0))],
        out_shape=[jax.ShapeDtypeStruct((8, n), F32),
                   jax.ShapeDtypeStruct((n, d), BF16)],
        scratch_shapes=[pltpu.VMEM((rows, HEAD_W), F32)],
        compiler_params=_cparams(("arbitrary",)),
        name="moe_router",
    )(x, w_hi, w_lo, b_col)


def _dispatch_kernel(seg_ref, x_ref, p1_ref, p2_ref, zeros_hbm, o_hbm, buf, sem):
    del zeros_hbm
    i = pl.program_id(0)
    n_steps = pl.num_programs(0)
    ts = x_ref.shape[0]
    rows = buf.shape[1]
    sizes = [SEG_ALIGN << b for b in range((ts // SEG_ALIGN).bit_length())]

    def segment_copies(step, slot, fn):
        for e in range(N_EXPERTS):
            length = seg_ref[step * 3 * N_EXPERTS + e]
            src = seg_ref[step * 3 * N_EXPERTS + N_EXPERTS + e]
            dst = seg_ref[step * 3 * N_EXPERTS + 2 * N_EXPERTS + e]
            done = 0 * length
            for size in sizes:
                copy = pltpu.make_async_copy(
                    buf.at[slot, pl.ds(pl.multiple_of(src + done, SEG_ALIGN), size), :],
                    o_hbm.at[pl.ds(pl.multiple_of(dst + done, SEG_ALIGN), size), :],
                    sem.at[slot, e])

                @pl.when((length & size) != 0)
                def _():
                    fn(copy)

                done = done + (length & size)

    slot = i % 2

    @pl.when(i >= 2)
    def _():
        segment_copies(i - 2, slot, lambda c: c.wait())

    rid = lax.broadcasted_iota(I32, (rows, ts), 0)
    onehot = jnp.where(p1_ref[...] == rid, 1.0, jnp.where(p2_ref[...] == rid, 1.0, 0.0)).astype(BF16)
    buf[slot] = _dot(onehot, x_ref[...]).astype(buf.dtype)
    segment_copies(i, slot, lambda c: c.start())

    @pl.when(i == n_steps - 1)
    def _():
        @pl.when(i >= 1)
        def _():
            segment_copies(i - 1, 1 - slot, lambda c: c.wait())

        segment_copies(i, slot, lambda c: c.wait())


def _dispatch(xb, pos1, pos2, seg, n_rows):
    n, d = xb.shape
    ts = min(SRC_TILE, n)
    n_src = n // ts
    rows = 2 * ts + N_EXPERTS * SEG_ALIGN
    pspec = pl.BlockSpec((None, 1, ts), lambda i, seg: (i, 0, 0))
    grid_spec = pltpu.PrefetchScalarGridSpec(
        num_scalar_prefetch=1,
        grid=(n_src,),
        in_specs=[pl.BlockSpec((ts, d), lambda i, seg: (i, 0)), pspec, pspec,
                  pl.BlockSpec(memory_space=pl.ANY)],
        out_specs=pl.BlockSpec(memory_space=pl.ANY),
        scratch_shapes=[pltpu.VMEM((2, rows, d), BF16), pltpu.SemaphoreType.DMA((2, N_EXPERTS))],
    )
    return pl.pallas_call(
        _dispatch_kernel,
        grid_spec=grid_spec,
        out_shape=jax.ShapeDtypeStruct((n_rows, d), BF16),
        input_output_aliases={4: 0},
        compiler_params=_cparams(("arbitrary",)),
        name="moe_dispatch",
    )(seg, xb, pos1.reshape(n_src, 1, ts), pos2.reshape(n_src, 1, ts), jnp.zeros((n_rows, d), BF16))


def _expert_kernel(be_ref, bv_ref, x_ref, w1_ref, w3_ref, w2_ref, o_ref):
    j = pl.program_id(0)

    @pl.when(bv_ref[j] > 0)
    def _():
        y = _swiglu_acc(x_ref[...], w1_ref, w3_ref, w2_ref, x_ref.shape[0], o_ref.shape[1])
        o_ref[...] = y.astype(o_ref.dtype)

    @pl.when(bv_ref[j] == 0)
    def _():
        o_ref[...] = jnp.zeros_like(o_ref)


def _experts(xs, w1, w3, w2, block_e, block_rows):
    n_rows, d = xs.shape
    f = w1.shape[2]
    assert f % FF_CHUNK == 0
    n_blocks = n_rows // EXPERT_BLOCK
    n_steps = block_e.shape[0]
    grid_spec = pltpu.PrefetchScalarGridSpec(
        num_scalar_prefetch=2,
        grid=(n_steps,),
        in_specs=[pl.BlockSpec((EXPERT_BLOCK, d), lambda j, be, bv: (jnp.minimum(j, n_blocks - 1), 0)),
                  pl.BlockSpec((None, d, f), lambda j, be, bv: (be[j], 0, 0)),
                  pl.BlockSpec((None, d, f), lambda j, be, bv: (be[j], 0, 0)),
                  pl.BlockSpec((None, f, d), lambda j, be, bv: (be[j], 0, 0))],
        out_specs=pl.BlockSpec((EXPERT_BLOCK, d), lambda j, be, bv: (j, 0)),
    )
    return pl.pallas_call(
        _expert_kernel,
        grid_spec=grid_spec,
        out_shape=jax.ShapeDtypeStruct((n_steps * EXPERT_BLOCK, d), BF16),
        compiler_params=_cparams(("arbitrary",)),
        name="moe_experts",
    )(block_e, block_rows, xs, w1, w3, w2)


def _combine_kernel(run_ref, x_ref, cm_ref, y_hbm, g_ref, beta_ref, o_ref, win, sem, *, alpha):
    i = pl.program_id(0)
    n_steps = pl.num_programs(0)
    tm = x_ref.shape[0]
    max_run = tm + COMB_ALIGN
    sizes = [COMB_ALIGN << b for b in range((max_run // COMB_ALIGN).bit_length())]

    def run_copies(step, slot, fn):
        for e in range(N_EXPERTS):
            length = run_ref[step * 3 * N_EXPERTS + e]
            src = run_ref[step * 3 * N_EXPERTS + N_EXPERTS + e]
            dst = run_ref[step * 3 * N_EXPERTS + 2 * N_EXPERTS + e]
            done = 0 * length
            for size in sizes:
                copy = pltpu.make_async_copy(
                    y_hbm.at[pl.ds(pl.multiple_of(src + done, COMB_ALIGN), size), :],
                    win.at[slot, pl.ds(pl.multiple_of(dst + done, COMB_ALIGN), size), :],
                    sem.at[slot, e])

                @pl.when((length & size) != 0)
                def _():
                    fn(copy)

                done = done + (length & size)

    @pl.when(i == 0)
    def _():
        win[...] = jnp.zeros_like(win)
        run_copies(0, 0, lambda c: c.start())

    slot = i % 2

    @pl.when(i + 1 < n_steps)
    def _():
        run_copies(i + 1, 1 - slot, lambda c: c.start())

    run_copies(i, slot, lambda c: c.wait())

    cm = cm_ref[...]
    col = lax.broadcasted_iota(I32, (tm, win.shape[1]), 1).astype(F32)
    p1 = jnp.where(col == cm[:, 0:1], 1.0, 0.0).astype(BF16)
    p2 = jnp.where(col == cm[:, 1:2], 1.0, 0.0).astype(BF16)
    rows = _dot(jnp.concatenate([p1, p2], axis=0), win[slot])
    y = cm[:, 2:3] * rows[:tm] + cm[:, 3:4] * rows[tm:]
    o_ref[...] = _layer_norm(alpha * x_ref[...] + y, g_ref[...], beta_ref[...])


def _combine_ln(x, cm, runs, yb, g, beta, alpha):
    n, d = x.shape
    tm = min(COMB_TILE, n)
    packed = TOP_K * tm + N_EXPERTS * 2 * COMB_ALIGN
    grid_spec = pltpu.PrefetchScalarGridSpec(
        num_scalar_prefetch=1,
        grid=(n // tm,),
        in_specs=[pl.BlockSpec((tm, d), lambda i, rr: (i, 0)),
                  pl.BlockSpec((tm, 8), lambda i, rr: (i, 0)),
                  pl.BlockSpec(memory_space=pl.ANY),
                  pl.BlockSpec((1, d), lambda i, rr: (0, 0)),
                  pl.BlockSpec((1, d), lambda i, rr: (0, 0))],
        out_specs=pl.BlockSpec((tm, d), lambda i, rr: (i, 0)),
        scratch_shapes=[pltpu.VMEM((2, packed, d), BF16),
                        pltpu.SemaphoreType.DMA((2, N_EXPERTS))],
    )
    return pl.pallas_call(
        functools.partial(_combine_kernel, alpha=alpha),
        grid_spec=grid_spec,
        out_shape=jax.ShapeDtypeStruct((n, d), F32),
        compiler_params=_cparams(("arbitrary",)),
        name="moe_combine_ln",
    )(runs, x, cm, yb, g[None, :], beta[None, :])


def _moe_ln(x, router_w, router_b, w1, w3, w2, g, beta, alpha):
    n, d = x.shape
    n_exp = router_w.shape[1]
    assert n_exp == N_EXPERTS
    pad = ROUTER_ROWS - n_exp
    w_t = jnp.pad(router_w.T, ((0, pad), (0, 0)))
    w_hi = w_t.astype(BF16)
    w_lo = (w_t - w_hi.astype(F32)).astype(BF16)
    b_col = jnp.broadcast_to(jnp.pad(router_b, (0, pad), constant_values=NEG_BIG)[:, None], (ROUTER_ROWS, HEAD_W))
    meta, xb = _router(x, w_hi, w_lo, b_col)

    e1, e2 = meta[0].astype(I32), meta[1].astype(I32)
    r1, r2 = meta[2].astype(I32), meta[3].astype(I32)
    experts = jnp.arange(n_exp, dtype=I32)[None, :]
    pick1 = (e1[:, None] == experts).astype(I32)
    pick2 = (e2[:, None] == experts).astype(I32)

    ct = min(COMB_TILE, n)
    ts = min(SRC_TILE, n)
    n_src = n // ts
    per_ct = jnp.sum((pick1 + pick2).reshape(n // ct, ct, n_exp), axis=1)
    before_ct = jnp.cumsum(per_ct, axis=0) - per_ct
    per_src = jnp.sum(per_ct.reshape(n_src, ts // ct, n_exp), axis=1)
    before_src = before_ct[:: ts // ct]
    seg_len = (per_src + SEG_ALIGN - 1) // SEG_ALIGN * SEG_ALIGN
    local_off = jnp.cumsum(seg_len, axis=1) - seg_len
    rows_e = jnp.sum(seg_len, axis=0)
    region = (rows_e + EXPERT_BLOCK - 1) // EXPERT_BLOCK * EXPERT_BLOCK
    region_end = jnp.cumsum(region)
    region_start = region_end - region
    seg_start = region_start[None, :] + jnp.cumsum(seg_len, axis=0) - seg_len
    n_blocks = (n * TOP_K + n_src * n_exp * (SEG_ALIGN - 1)) // EXPERT_BLOCK + 1 + n_exp
    n_rows = n_blocks * EXPERT_BLOCK

    def per_token(table, pick):
        tiles = table.shape[0]
        tok = jnp.broadcast_to(table[:, None, :], (tiles, n // tiles, n_exp)).reshape(n, n_exp)
        return jnp.sum(pick * tok, axis=1)

    in_tile1 = r1 - per_token(before_src, pick1)
    in_tile2 = r2 - per_token(before_src, pick2)
    pos1 = per_token(local_off, pick1) + in_tile1
    pos2 = per_token(local_off, pick2) + in_tile2
    dest1 = per_token(seg_start, pick1) + in_tile1
    dest2 = per_token(seg_start, pick2) + in_tile2
    seg = jnp.concatenate([seg_len, local_off, seg_start], axis=1).reshape(-1).astype(I32)
    xs = _dispatch(xb, pos1, pos2, seg, n_rows)

    bstart = jnp.arange(n_blocks + 1, dtype=I32) * EXPERT_BLOCK
    block_e = jnp.minimum(jnp.sum((bstart[:, None] >= region_end[None, :]).astype(I32), axis=1), n_exp - 1)
    block_rows = jnp.clip((region_start + rows_e)[block_e] - bstart, 0, EXPERT_BLOCK).astype(I32)
    yb = _experts(xs, w1, w3, w2, block_e, block_rows)

    first_ct = jnp.repeat(seg_start - before_src, ts // ct, axis=0) + before_ct
    run_start = first_ct // COMB_ALIGN * COMB_ALIGN
    run_len = (first_ct + per_ct - run_start + COMB_ALIGN - 1) // COMB_ALIGN * COMB_ALIGN
    run_len = jnp.where(per_ct > 0, run_len, 0)
    run_off = jnp.cumsum(run_len, axis=1) - run_len
    c1 = per_token(run_off - run_start, pick1) + dest1
    c2 = per_token(run_off - run_start, pick2) + dest2
    runs = jnp.concatenate([run_len, run_start, run_off], axis=1).reshape(-1).astype(I32)
    cm = jnp.stack([c1.astype(F32), c2.astype(F32), meta[4], meta[5]]
                   + [jnp.zeros((n,), F32)] * 4, axis=1)
    return _combine_ln(x, cm, runs, yb, g, beta, alpha)


def kernel(x, mem, ln1_g, ln1_b, lnx_g, lnx_b, ln2_g, ln2_b, xa_wq, xa_wkv, xa_wo, attn_w_in, attn_w_o, diff_lq1, diff_lk1, diff_lq2, diff_lk2, diff_subln_g, ffn_w1, ffn_w3, ffn_w2, conv_w_pw1, conv_b_pw1, conv_dw_w, conv_dw_b, conv_ln_g, conv_ln_b, conv_w_pw2, conv_b_pw2, moe_router_w, moe_router_b, moe_w1, moe_w3, moe_w2):
    batch, seq, d = x.shape
    n_mem = mem.shape[1]
    depth = ln1_g.shape[0]
    alpha = (2 * depth) ** 0.25
    sb_width = SB_HEADS * HEAD_W

    bf = lambda w: w.astype(BF16)
    cos_t, sa_t, sb_t = _rope_tables(seq)
    h = x.reshape(batch * seq, d)
    mem2 = mem.reshape(batch * n_mem, d)

    for i in range(depth):
        j = i // 2
        if i % 2 == 0:
            lambda_init = 0.8 - 0.6 * math.exp(-0.3 * i)
            qkv = _inproj(h, bf(attn_w_in[j]), cos_t, sa_t, sb_t, seq)
            o_sb = _sb_attention(qkv, batch, seq)
            o_df = _diff_attention(qkv, diff_lq1[j], diff_lk1[j], diff_lq2[j], diff_lk2[j],
                                   diff_subln_g[j], batch, seq, lambda_init)
            w_o = bf(attn_w_o[j])
            h = _proj2_ln(o_sb, o_df, w_o[:sb_width], w_o[sb_width:], h, ln1_g[i], ln1_b[i], alpha)
        else:
            h = _conv_ln(h, bf(conv_w_pw1[j]), conv_b_pw1[j], conv_dw_w[j], conv_dw_b[j],
                         conv_ln_g[j], conv_ln_b[j], bf(conv_w_pw2[j]), conv_b_pw2[j],
                         ln1_g[i], ln1_b[i], seq, alpha)
        kv = _matmul(mem2, bf(xa_wkv[i]), "xattn_kv")
        h = _xattn_ln(h, kv, bf(xa_wq[i]), bf(xa_wo[i]), lnx_g[i], lnx_b[i], seq, n_mem, alpha)
        if i % 2 == 0:
            h = _ffn_ln(h, bf(ffn_w1[j]), bf(ffn_w3[j]), bf(ffn_w2[j]), ln2_g[i], ln2_b[i], alpha)
        else:
            h = _moe_ln(h, moe_router_w[j], moe_router_b[j], bf(moe_w1[j]), bf(moe_w3[j]), bf(moe_w2[j]),
                        ln2_g[i], ln2_b[i], alpha)
    return h.reshape(batch, seq, d)
```

```python
import functools
import math

import jax
import jax.numpy as jnp
from jax import lax
from jax.experimental import pallas as pl
from jax.experimental.pallas import tpu as pltpu

F32 = jnp.float32
BF16 = jnp.bfloat16
I32 = jnp.int32

LN_EPS = 1e-5
NEG_BIG = -1e30

CHUNK = 64
SB_HEADS = 4
HEAD_W = 128
DIFF_HEADS = 4
DIFF_HEAD_DIM = 64
ROPE_THETA = 10000.0
CONV_WIDTH = 31
CONV_HALO = 32
XA_HEADS = 4
N_EXPERTS = 8
TOP_K = 2

VMEM_LIMIT = 56 * 1024 * 1024
ROW_TILE = 512
SB_TQ = 1024
SB_TK = 256
SB_UNROLL = 4
DIFF_TQ = 512
DIFF_TK = 512
LOG2E = 1.4426950408889634
FF_CHUNK = 256
CONV_UNROLL = 4
CONV_ROWS = 8
ROUTER_TILE = 512
ROUTER_ROWS = 16
SRC_TILE = 512
SEG_ALIGN = 16
EXPERT_BLOCK = 512
COMB_TILE = 128
COMB_ALIGN = 16
XA_TILE = 1024


def _cparams(sem):
    return pltpu.CompilerParams(dimension_semantics=sem, vmem_limit_bytes=VMEM_LIMIT)


def _layer_norm(y, g, b):
    mu = jnp.mean(y, axis=-1, keepdims=True)
    d = y - mu
    var = jnp.mean(d * d, axis=-1, keepdims=True)
    return d * lax.rsqrt(var + LN_EPS) * g + b


def _sigmoid(x):
    return 1.0 / (1.0 + jnp.exp(-x))


def _neg_abs(x):
    bits = lax.bitcast_convert_type(x, jnp.uint32) | jnp.uint32(0x80000000)
    return lax.bitcast_convert_type(bits, F32)


def _dot(a, b):
    return jnp.dot(a, b, preferred_element_type=F32)


def _dot_nt(a, b):
    return lax.dot_general(a, b, (((1,), (1,)), ((), ())), preferred_element_type=F32)


def _const_spec(shape):
    nd = len(shape)
    return pl.BlockSpec(shape, lambda *_: (0,) * nd)


def _matmul_kernel(x_ref, w_ref, o_ref):
    o_ref[...] = _dot(x_ref[...].astype(BF16), w_ref[...]).astype(o_ref.dtype)


def _matmul(x, w, name):
    n, d = x.shape
    m = w.shape[1]
    tm = min(ROW_TILE, n)
    return pl.pallas_call(
        _matmul_kernel,
        grid=(n // tm,),
        in_specs=[pl.BlockSpec((tm, d), lambda i: (i, 0)), _const_spec((d, m))],
        out_specs=pl.BlockSpec((tm, m), lambda i: (i, 0)),
        out_shape=jax.ShapeDtypeStruct((n, m), BF16),
        compiler_params=_cparams(("parallel",)),
        name=name,
    )(x, w)


def _inproj_kernel(x_ref, w_ref, cos_ref, sa_ref, sb_ref, o_ref, *, width, sq_scale, dq_scale):
    xb = x_ref[...].astype(BF16)
    n_groups = o_ref.shape[1] // width
    for j in range(n_groups):
        acc = _dot(xb, w_ref[:, j * width:(j + 1) * width])
        if j in (3, 4):
            parts = []
            for c in range(width // HEAD_W):
                hc = acc[:, c * HEAD_W:(c + 1) * HEAD_W]
                rot = (hc * cos_ref[...]
                       + pltpu.roll(hc, HEAD_W - DIFF_HEAD_DIM // 2, 1) * sa_ref[...]
                       + pltpu.roll(hc, DIFF_HEAD_DIM // 2, 1) * sb_ref[...])
                parts.append(rot)
            acc = jnp.concatenate(parts, axis=1)
            if j == 3:
                acc = acc * dq_scale
        if j == 0:
            acc = acc * sq_scale
        o_ref[:, j * width:(j + 1) * width] = acc.astype(o_ref.dtype)


def _inproj(x, w, cos_t, sa_t, sb_t, seq):
    n, d = x.shape
    m = w.shape[1]
    tm = min(ROW_TILE, seq)
    per_seq = seq // tm
    tab = pl.BlockSpec((tm, HEAD_W), lambda i: (i % per_seq, 0))
    return pl.pallas_call(
        functools.partial(_inproj_kernel, width=m // 6, sq_scale=HEAD_W ** -0.5 * LOG2E,
                          dq_scale=DIFF_HEAD_DIM ** -0.5 * LOG2E),
        grid=(n // tm,),
        in_specs=[pl.BlockSpec((tm, d), lambda i: (i, 0)), _const_spec((d, m)), tab, tab, tab],
        out_specs=pl.BlockSpec((tm, m), lambda i: (i, 0)),
        out_shape=jax.ShapeDtypeStruct((n, m), BF16),
        compiler_params=_cparams(("parallel",)),
        name="attn_inproj",
    )(x, w, cos_t, sa_t, sb_t)


def _rope_tables(seq):
    half = DIFF_HEAD_DIM // 2
    inv = ROPE_THETA ** (-jnp.arange(0, DIFF_HEAD_DIM, 2, dtype=F32) / DIFF_HEAD_DIM)
    ang = jnp.arange(seq, dtype=F32)[:, None] * inv[None, :]
    cos, sin = jnp.cos(ang), jnp.sin(ang)
    zero = jnp.zeros_like(sin)
    reps = HEAD_W // DIFF_HEAD_DIM
    cos_t = jnp.tile(jnp.concatenate([cos, cos], axis=1), (1, reps))
    sa_t = jnp.tile(jnp.concatenate([-sin, zero], axis=1), (1, reps))
    sb_t = jnp.tile(jnp.concatenate([zero, sin], axis=1), (1, reps))
    del half
    return cos_t, sa_t, sb_t


def _sb_kernel(q_ref, k_ref, v_ref, o_ref, acc_ref, carry_ref, za_ref, zb_ref, *, tq, tk):
    i = pl.program_id(2)
    band = tq // tk
    row = lax.broadcasted_iota(I32, (tk, tk), 0)
    col = lax.broadcasted_iota(I32, (tk, tk), 1)
    suffix = jnp.where(row >= col, 1.0, 0.0).astype(BF16)

    def logits(kb):
        k0 = pl.multiple_of(kb * tk, tk)
        return _dot_nt(q_ref[...], k_ref[pl.ds(k0, tk), :])

    def rest(z, kb, rows, strict=None):
        k0 = pl.multiple_of(kb * tk, tk)
        drop = jnp.maximum(z, 0.0) + jnp.log(1.0 + jnp.exp2(_neg_abs(z))) * LOG2E
        if strict is not None:
            drop = jnp.where(strict, drop, 0.0)
        hi = drop.astype(BF16)
        lo = (drop - hi.astype(F32)).astype(BF16)
        within = _dot(hi, suffix) + _dot(lo, suffix)
        w = jnp.exp2(z - within - carry_ref[rows, :])
        if strict is not None:
            w = jnp.where(strict, w, 0.0)
        acc_ref[rows, :] += _dot(w.astype(BF16), v_ref[pl.ds(k0, tk), :])
        carry_ref[rows, :] += within[:, 0:1]

    acc_ref[...] = jnp.zeros_like(acc_ref)
    carry_ref[...] = jnp.zeros_like(carry_ref)
    strict = col < row
    for c in reversed(range(band)):
        kb = i * band + c
        k0 = pl.multiple_of(kb * tk, tk)
        diag = slice(c * tk, (c + 1) * tk)
        rest(_dot_nt(q_ref[diag, :], k_ref[pl.ds(k0, tk), :]), kb, diag, strict)
        if c + 1 < band:
            lower = slice((c + 1) * tk, tq)
            rest(_dot_nt(q_ref[lower, :], k_ref[pl.ds(k0, tk), :]), kb, lower)
    every = slice(0, tq)
    below = i * band

    zs = (za_ref, zb_ref)
    rem = below % SB_UNROLL

    def single(t, c):
        kb = below - 1 - t
        rest(logits(kb), kb, every)
        return c

    lax.fori_loop(0, rem, single, 0)
    top = below - rem
    za_ref[...] = logits(jnp.maximum(top - 1, 0))

    def body(t, c):
        kb = top - 1 - SB_UNROLL * t
        for u in range(SB_UNROLL):
            zs[(u + 1) % 2][...] = logits(jnp.maximum(kb - u - 1, 0))
            rest(zs[u % 2][...], kb - u, every)
        return c

    lax.fori_loop(0, top // SB_UNROLL, body, 0)
    o_ref[...] = acc_ref[...].astype(o_ref.dtype)


def _whole_seq_spec(seq, first):
    return pl.BlockSpec((seq, HEAD_W), lambda b, h, i: (b, first + h), pipeline_mode=pl.Buffered(1))


def _sb_attention(qkv, batch, seq):
    tq = min(SB_TQ, seq)
    tk = min(SB_TK, seq)
    assert SB_UNROLL % 2 == 0 and tq % tk == 0
    nq = seq // tq
    return pl.pallas_call(
        functools.partial(_sb_kernel, tq=tq, tk=tk),
        grid=(batch, SB_HEADS, nq),
        in_specs=[
            pl.BlockSpec((tq, HEAD_W), lambda b, h, i: (b * nq + i, h)),
            _whole_seq_spec(seq, SB_HEADS), _whole_seq_spec(seq, 2 * SB_HEADS),
        ],
        out_specs=pl.BlockSpec((tq, HEAD_W), lambda b, h, i: (b * nq + i, h)),
        out_shape=jax.ShapeDtypeStruct((batch * seq, SB_HEADS * HEAD_W), BF16),
        scratch_shapes=[pltpu.VMEM((tq, HEAD_W), F32), pltpu.VMEM((tq, 1), F32),
                        pltpu.VMEM((tq, tk), F32), pltpu.VMEM((tq, tk), F32)],
        compiler_params=_cparams(("parallel", "parallel", "arbitrary")),
        name="stickbreak_attn",
    )(qkv, qkv, qkv)


def _diff_kernel(lq1_ref, lk1_ref, lq2_ref, lk2_ref, g_ref, q_ref, k_ref, v_ref, o_ref,
                 qq_ref, m_ref, l_ref, acc_ref, za_ref, zb_ref, *, tq, tk, lambda_init):
    i = pl.program_id(2)
    q = q_ref[...]
    lane = lax.broadcasted_iota(I32, (tq, HEAD_W), 1)
    zero = jnp.zeros_like(q)
    qq_ref[0:tq, :] = jnp.where(lane < DIFF_HEAD_DIM, q, zero)
    qq_ref[tq:2 * tq, :] = jnp.where(lane >= DIFF_HEAD_DIM, q, zero)
    m_ref[...] = jnp.full(m_ref.shape, NEG_BIG, F32)
    l_ref[...] = jnp.zeros_like(l_ref)
    acc_ref[...] = jnp.zeros_like(acc_ref)

    def logits(kb):
        k0 = pl.multiple_of(kb * tk, tk)
        return _dot_nt(qq_ref[...], k_ref[pl.ds(k0, tk), :])

    ones = jnp.ones((tk, HEAD_W), BF16)
    reps = tk // HEAD_W

    def update(z, kb):
        k0 = pl.multiple_of(kb * tk, tk)
        m_old = m_ref[...]
        m_new = jnp.maximum(m_old, jnp.max(z, axis=1, keepdims=True))
        p = jnp.exp2(z - jnp.concatenate([m_new] * reps, axis=1))
        a = jnp.exp2(m_old - m_new)
        pv = _dot(p.astype(BF16), jnp.concatenate([v_ref[pl.ds(k0, tk), :], ones], axis=1))
        l_ref[...] = a * l_ref[...] + pv[:, HEAD_W:]
        acc_ref[...] = a * acc_ref[...] + pv[:, :HEAD_W]
        m_ref[...] = m_new

    row = lax.broadcasted_iota(I32, (2 * tq, tk), 0)
    col = lax.broadcasted_iota(I32, (2 * tq, tk), 1)
    q_pos = jnp.where(row >= tq, row - tq, row)
    allowed = (col // CHUNK) <= (q_pos // CHUNK)
    za_ref[...] = jnp.where(allowed, logits(i), NEG_BIG)

    def body(t, c):
        kb = i - 2 * t
        zb_ref[...] = logits(kb - 1)
        update(za_ref[...], kb)
        za_ref[...] = logits(jnp.maximum(kb - 2, 0))
        update(zb_ref[...], kb - 1)
        return c

    lax.fori_loop(0, (i + 1) // 2, body, 0)

    @pl.when(i % 2 == 0)
    def _():
        update(za_ref[...], 0)

    lam = (jnp.exp(jnp.sum(lq1_ref[...] * lk1_ref[...], axis=1, keepdims=True))
           - jnp.exp(jnp.sum(lq2_ref[...] * lk2_ref[...], axis=1, keepdims=True)) + lambda_init)
    o = acc_ref[0:tq, :] / l_ref[0:tq, :] - lam * (acc_ref[tq:2 * tq, :] / l_ref[tq:2 * tq, :])
    rms = lax.rsqrt(jnp.mean(o * o, axis=-1, keepdims=True) + LN_EPS)
    o_ref[...] = (o * rms * g_ref[...] * (1.0 - lambda_init)).astype(o_ref.dtype)


def _diff_attention(qkv, lq1, lk1, lq2, lk2, subln_g, batch, seq, lambda_init):
    tq = min(DIFF_TQ, seq)
    tk = min(DIFF_TK, seq)
    assert tq == tk and tk % CHUNK == 0
    nq = seq // tq
    base = 3 * SB_HEADS
    vec = _const_spec((1, DIFF_HEAD_DIM))
    return pl.pallas_call(
        functools.partial(_diff_kernel, tq=tq, tk=tk, lambda_init=lambda_init),
        grid=(batch, DIFF_HEADS, nq),
        in_specs=[
            vec, vec, vec, vec, _const_spec((1, HEAD_W)),
            pl.BlockSpec((tq, HEAD_W), lambda b, h, i: (b * nq + i, base + h)),
            _whole_seq_spec(seq, base + DIFF_HEADS), _whole_seq_spec(seq, base + 2 * DIFF_HEADS),
        ],
        out_specs=pl.BlockSpec((tq, HEAD_W), lambda b, h, i: (b * nq + i, h)),
        out_shape=jax.ShapeDtypeStruct((batch * seq, DIFF_HEADS * HEAD_W), BF16),
        scratch_shapes=[pltpu.VMEM((2 * tq, HEAD_W), BF16), pltpu.VMEM((2 * tq, HEAD_W), F32),
                        pltpu.VMEM((2 * tq, HEAD_W), F32), pltpu.VMEM((2 * tq, HEAD_W), F32),
                        pltpu.VMEM((2 * tq, tk), F32), pltpu.VMEM((2 * tq, tk), F32)],
        compiler_params=_cparams(("parallel", "parallel", "arbitrary")),
        name="diff_attn",
    )(lq1[None, :], lk1[None, :], lq2[None, :], lk2[None, :], subln_g[None, :], qkv, qkv, qkv)


def _proj2_ln_kernel(a_ref, b_ref, wa_ref, wb_ref, x_ref, g_ref, beta_ref, o_ref, *, alpha):
    y = _dot(a_ref[...], wa_ref[...]) + _dot(b_ref[...], wb_ref[...])
    o_ref[...] = _layer_norm(alpha * x_ref[...] + y, g_ref[...], beta_ref[...])


def _proj2_ln(a, b, wa, wb, x, g, beta, alpha):
    n, d = x.shape
    tm = min(ROW_TILE, n)
    ka, kb = a.shape[1], b.shape[1]
    row = lambda w: pl.BlockSpec((tm, w), lambda i: (i, 0))
    return pl.pallas_call(
        functools.partial(_proj2_ln_kernel, alpha=alpha),
        grid=(n // tm,),
        in_specs=[row(ka), row(kb), _const_spec((ka, d)), _const_spec((kb, d)), row(d),
                  _const_spec((1, d)), _const_spec((1, d))],
        out_specs=row(d),
        out_shape=jax.ShapeDtypeStruct((n, d), F32),
        compiler_params=_cparams(("parallel",)),
        name="attn_outproj_ln",
    )(a, b, wa, wb, x, g[None, :], beta[None, :])


def _xattn_kernel(x_ref, kv_ref, wq_ref, wo_ref, g_ref, beta_ref, o_ref, *, alpha, scale):
    x = x_ref[...]
    q = _dot(x.astype(BF16), wq_ref[...]).astype(BF16)
    width = wq_ref.shape[1]
    outs = []
    for h in range(XA_HEADS):
        qh = q[:, h * HEAD_W:(h + 1) * HEAD_W]
        kh = kv_ref[:, h * HEAD_W:(h + 1) * HEAD_W]
        vh = kv_ref[:, width + h * HEAD_W:width + (h + 1) * HEAD_W]
        z = _dot_nt(qh, kh) * scale
        z = z - jnp.max(z, axis=1, keepdims=True)
        p = jnp.exp(z)
        p = p / jnp.sum(p, axis=1, keepdims=True)
        outs.append(_dot(p.astype(BF16), vh))
    o = jnp.concatenate(outs, axis=1).astype(BF16)
    y = _dot(o, wo_ref[...])
    o_ref[...] = _layer_norm(alpha * x + y, g_ref[...], beta_ref[...])


def _xattn_ln(x, kv, wq, wo, g, beta, seq, n_mem, alpha):
    n, d = x.shape
    tm = min(XA_TILE, seq)
    per_seq = seq // tm
    width = wq.shape[1]
    return pl.pallas_call(
        functools.partial(_xattn_kernel, alpha=alpha, scale=HEAD_W ** -0.5),
        grid=(n // tm,),
        in_specs=[pl.BlockSpec((tm, d), lambda i: (i, 0)),
                  pl.BlockSpec((n_mem, 2 * width), lambda i: (i // per_seq, 0)),
                  _const_spec((d, width)), _const_spec((width, d)),
                  _const_spec((1, d)), _const_spec((1, d))],
        out_specs=pl.BlockSpec((tm, d), lambda i: (i, 0)),
        out_shape=jax.ShapeDtypeStruct((n, d), F32),
        compiler_params=_cparams(("parallel",)),
        name="xattn_ln",
    )(x, kv, wq, wo, g[None, :], beta[None, :])


def _swiglu_acc(xb, w1_ref, w3_ref, w2_ref, rows, d_out):
    d_ff = w1_ref.shape[1]
    acc = jnp.zeros((rows, d_out), F32)
    for c in range(d_ff // FF_CHUNK):
        sl = slice(c * FF_CHUNK, (c + 1) * FF_CHUNK)
        a = _dot(xb, w1_ref[:, sl])
        b = _dot(xb, w3_ref[:, sl])
        h = (a * _sigmoid(a) * b).astype(BF16)
        acc = acc + _dot(h, w2_ref[sl, :])
    return acc


def _ffn_kernel(x_ref, w1_ref, w3_ref, w2_ref, g_ref, beta_ref, o_ref, *, alpha):
    x = x_ref[...]
    y = _swiglu_acc(x.astype(BF16), w1_ref, w3_ref, w2_ref, x.shape[0], x.shape[1])
    o_ref[...] = _layer_norm(alpha * x + y, g_ref[...], beta_ref[...])


def _ffn_ln(x, w1, w3, w2, g, beta, alpha):
    n, d = x.shape
    f = w1.shape[1]
    assert f % FF_CHUNK == 0
    tm = min(ROW_TILE, n)
    return pl.pallas_call(
        functools.partial(_ffn_kernel, alpha=alpha),
        grid=(n // tm,),
        in_specs=[pl.BlockSpec((tm, d), lambda i: (i, 0)),
                  _const_spec((d, f)), _const_spec((d, f)), _const_spec((f, d)),
                  _const_spec((1, d)), _const_spec((1, d))],
        out_specs=pl.BlockSpec((tm, d), lambda i: (i, 0)),
        out_shape=jax.ShapeDtypeStruct((n, d), F32),
        compiler_params=_cparams(("parallel",)),
        name="ffn_ln",
    )(x, w1, w3, w2, g[None, :], beta[None, :])


def _conv_kernel(x_ref, xp_ref, wpw1_ref, bpw1_ref, dww_ref, dwb_ref, cg_ref, cb_ref,
                 wpw2_ref, bpw2_ref, g_ref, beta_ref, o_ref, h_scr, c_scr, *, alpha, per_seq):
    i = pl.program_id(0)
    tm, ch = x_ref.shape

    def glu(xv):
        xb = xv.astype(BF16)
        a = _dot(xb, wpw1_ref[:, :ch]) + bpw1_ref[:, :ch]
        g = _dot(xb, wpw1_ref[:, ch:]) + bpw1_ref[:, ch:]
        return a * _sigmoid(g)

    halo = glu(xp_ref[...])
    h_scr[0:CONV_HALO, :] = jnp.where(i % per_seq == 0, 0.0, halo)
    x = x_ref[...]
    h_scr[CONV_HALO:CONV_HALO + tm, :] = glu(x)
    h_scr[CONV_HALO + tm:CONV_HALO + tm + CONV_ROWS, :] = jnp.zeros((CONV_ROWS, ch), F32)

    first = CONV_HALO - (CONV_WIDTH - 1)
    sub = lax.broadcasted_iota(I32, (CONV_ROWS, HEAD_W), 0)

    for lt in range(ch // HEAD_W):
        lanes = slice(lt * HEAD_W, (lt + 1) * HEAD_W)

        def partials(base):
            out = []
            for s in range(CONV_ROWS):
                acc = None
                for a in range((first + CONV_WIDTH - 1) // CONV_ROWS + 1):
                    k = CONV_ROWS * a + s - first
                    if 0 <= k < CONV_WIDTH:
                        r = pl.multiple_of(base + CONV_ROWS * a, CONV_ROWS)
                        term = h_scr[pl.ds(r, CONV_ROWS), lanes] * dww_ref[k:k + 1, lanes]
                        acc = term if acc is None else acc + term
                out.append(acc)
            return tuple(out)

        def one_block(r0, g_old):
            g_new = partials(r0 + CONV_ROWS)
            acc = g_old[0]
            for s in range(1, CONV_ROWS):
                acc = acc + pltpu.roll(jnp.where(sub >= s, g_old[s], g_new[s]), CONV_ROWS - s, 0)
            c_scr[pl.ds(r0, CONV_ROWS), lanes] = acc
            return g_new

        def rows(c, g):
            r0 = pl.multiple_of(c * (CONV_UNROLL * CONV_ROWS), CONV_UNROLL * CONV_ROWS)
            for u in range(CONV_UNROLL):
                g = one_block(r0 + u * CONV_ROWS, g)
            return g

        lax.fori_loop(0, tm // (CONV_UNROLL * CONV_ROWS), rows, partials(0))

    y = _layer_norm(c_scr[...] + dwb_ref[...], cg_ref[...], cb_ref[...])
    y = _dot((y * _sigmoid(y)).astype(BF16), wpw2_ref[...]) + bpw2_ref[...]
    o_ref[...] = _layer_norm(alpha * x + y, g_ref[...], beta_ref[...])


def _conv_ln(x, wpw1, bpw1, dww, dwb, cg, cb, wpw2, bpw2, g, beta, seq, alpha):
    n, d = x.shape
    tm = min(ROW_TILE, seq)
    per_seq = seq // tm
    halo_per_tile = tm // CONV_HALO
    vec = lambda w: _const_spec((1, w))
    return pl.pallas_call(
        functools.partial(_conv_kernel, alpha=alpha, per_seq=per_seq),
        grid=(n // tm,),
        in_specs=[pl.BlockSpec((tm, d), lambda i: (i, 0)),
                  pl.BlockSpec((CONV_HALO, d), lambda i: (jnp.maximum(i * halo_per_tile - 1, 0), 0)),
                  _const_spec((d, 2 * d)), vec(2 * d), _const_spec((CONV_WIDTH, d)), vec(d), vec(d), vec(d),
                  _const_spec((d, d)), vec(d), vec(d), vec(d)],
        out_specs=pl.BlockSpec((tm, d), lambda i: (i, 0)),
        out_shape=jax.ShapeDtypeStruct((n, d), F32),
        scratch_shapes=[pltpu.VMEM((tm + CONV_HALO + CONV_ROWS, d), F32), pltpu.VMEM((tm, d), F32)],
        compiler_params=_cparams(("parallel",)),
        name="conv_module_ln",
    )(x, x, wpw1, bpw1[None, :], dww, dwb[None, :], cg[None, :], cb[None, :],
      wpw2, bpw2[None, :], g[None, :], beta[None, :])


def _router_kernel(x_ref, wh_ref, wl_ref, b_ref, meta_ref, xb_ref, count_scr):
    i = pl.program_id(0)
    tm = x_ref.shape[0]
    rows = wh_ref.shape[0]

    @pl.when(i == 0)
    def _():
        count_scr[...] = jnp.zeros_like(count_scr)

    x = x_ref[...]
    x_hi = x.astype(BF16)
    x_lo = (x - x_hi.astype(F32)).astype(BF16)
    xb_ref[...] = x_hi
    logits = (_dot_nt(wh_ref[...], x_hi) + _dot_nt(wh_ref[...], x_lo) + _dot_nt(wl_ref[...], x_hi)
              + b_ref[:, 0:1])
    sub = lax.broadcasted_iota(I32, (rows, tm), 0)
    m1 = jnp.max(logits, axis=0, keepdims=True)
    i1 = jnp.min(jnp.where(logits == m1, sub, rows), axis=0, keepdims=True)
    rest = jnp.where(sub == i1, NEG_BIG * 2, logits)
    m2 = jnp.max(rest, axis=0, keepdims=True)
    i2 = jnp.min(jnp.where(rest == m2, sub, rows), axis=0, keepdims=True)
    t = jnp.exp(m2 - m1)
    g1 = 1.0 / (1.0 + t)
    g2 = t / (1.0 + t)

    pick1 = sub == i1
    pick2 = sub == i2
    chosen = jnp.where(pick1, 1.0, jnp.where(pick2, 1.0, 0.0))
    r = lax.broadcasted_iota(I32, (tm, tm), 0)
    c = lax.broadcasted_iota(I32, (tm, tm), 1)
    earlier = jnp.where(r < c, 1.0, 0.0).astype(BF16)
    before = count_scr[:, 0:1]
    rank = _dot(chosen.astype(BF16), earlier) + before
    r1 = jnp.sum(jnp.where(pick1, rank, 0.0), axis=0, keepdims=True)
    r2 = jnp.sum(jnp.where(pick2, rank, 0.0), axis=0, keepdims=True)

    out_row = lax.broadcasted_iota(I32, meta_ref.shape, 0)
    meta = jnp.where(out_row == 0, i1.astype(F32), 0.0)
    meta = jnp.where(out_row == 1, i2.astype(F32), meta)
    meta = jnp.where(out_row == 2, r1, meta)
    meta = jnp.where(out_row == 3, r2, meta)
    meta = jnp.where(out_row == 4, g1, meta)
    meta = jnp.where(out_row == 5, g2, meta)
    meta_ref[...] = meta

    total = before + jnp.sum(chosen, axis=1, keepdims=True)
    count_scr[...] = jnp.broadcast_to(total, count_scr.shape)


def _router(x, w_hi, w_lo, b_col):
    n, d = x.shape
    tm = min(ROUTER_TILE, n)
    rows = w_hi.shape[0]
    return pl.pallas_call(
        _router_kernel,
        grid=(n // tm,),
        in_specs=[pl.BlockSpec((tm, d), lambda i: (i, 0)), _const_spec((rows, d)), _const_spec((rows, d)),
                  _const_spec((rows, HEAD_W))],
        out_specs=[pl.BlockSpec((8, tm), lambda i: (0, i)),
                   pl.BlockSpec((tm, d), lambda i: (i, 0))],
        out_shape=[jax.ShapeDtypeStruct((8, n), F32),
                   jax.ShapeDtypeStruct((n, d), BF16)],
        scratch_shapes=[pltpu.VMEM((rows, HEAD_W), F32)],
        compiler_params=_cparams(("arbitrary",)),
        name="moe_router",
    )(x, w_hi, w_lo, b_col)


def _dispatch_kernel(seg_ref, x_ref, p1_ref, p2_ref, zeros_hbm, o_hbm, buf, sem):
    del zeros_hbm
    i = pl.program_id(0)
    n_steps = pl.num_programs(0)
    ts = x_ref.shape[0]
    rows = buf.shape[1]
    sizes = [SEG_ALIGN << b for b in range((ts // SEG_ALIGN).bit_length())]

    def segment_copies(step, slot, fn):
        for e in range(N_EXPERTS):
            length = seg_ref[step * 3 * N_EXPERTS + e]
            src = seg_ref[step * 3 * N_EXPERTS + N_EXPERTS + e]
            dst = seg_ref[step * 3 * N_EXPERTS + 2 * N_EXPERTS + e]
            done = 0 * length
            for size in sizes:
                copy = pltpu.make_async_copy(
                    buf.at[slot, pl.ds(pl.multiple_of(src + done, SEG_ALIGN), size), :],
                    o_hbm.at[pl.ds(pl.multiple_of(dst + done, SEG_ALIGN), size), :],
                    sem.at[slot, e])

                @pl.when((length & size) != 0)
                def _():
                    fn(copy)

                done = done + (length & size)

    slot = i % 2

    @pl.when(i >= 2)
    def _():
        segment_copies(i - 2, slot, lambda c: c.wait())

    rid = lax.broadcasted_iota(I32, (rows, ts), 0)
    onehot = jnp.where(p1_ref[...] == rid, 1.0, jnp.where(p2_ref[...] == rid, 1.0, 0.0)).astype(BF16)
    buf[slot] = _dot(onehot, x_ref[...]).astype(buf.dtype)
    segment_copies(i, slot, lambda c: c.start())

    @pl.when(i == n_steps - 1)
    def _():
        @pl.when(i >= 1)
        def _():
            segment_copies(i - 1, 1 - slot, lambda c: c.wait())

        segment_copies(i, slot, lambda c: c.wait())


def _dispatch(xb, pos1, pos2, seg, n_rows):
    n, d = xb.shape
    ts = min(SRC_TILE, n)
    n_src = n // ts
    rows = 2 * ts + N_EXPERTS * SEG_ALIGN
    pspec = pl.BlockSpec((None, 1, ts), lambda i, seg: (i, 0, 0))
    grid_spec = pltpu.PrefetchScalarGridSpec(
        num_scalar_prefetch=1,
        grid=(n_src,),
        in_specs=[pl.BlockSpec((ts, d), lambda i, seg: (i, 0)), pspec, pspec,
                  pl.BlockSpec(memory_space=pl.ANY)],
        out_specs=pl.BlockSpec(memory_space=pl.ANY),
        scratch_shapes=[pltpu.VMEM((2, rows, d), BF16), pltpu.SemaphoreType.DMA((2, N_EXPERTS))],
    )
    return pl.pallas_call(
        _dispatch_kernel,
        grid_spec=grid_spec,
        out_shape=jax.ShapeDtypeStruct((n_rows, d), BF16),
        input_output_aliases={4: 0},
        compiler_params=_cparams(("arbitrary",)),
        name="moe_dispatch",
    )(seg, xb, pos1.reshape(n_src, 1, ts), pos2.reshape(n_src, 1, ts), jnp.zeros((n_rows, d), BF16))


def _expert_kernel(be_ref, bv_ref, x_ref, w1_ref, w3_ref, w2_ref, o_ref):
    j = pl.program_id(0)

    @pl.when(bv_ref[j] > 0)
    def _():
        y = _swiglu_acc(x_ref[...], w1_ref, w3_ref, w2_ref, x_ref.shape[0], o_ref.shape[1])
        o_ref[...] = y.astype(o_ref.dtype)

    @pl.when(bv_ref[j] == 0)
    def _():
        o_ref[...] = jnp.zeros_like(o_ref)


def _experts(xs, w1, w3, w2, block_e, block_rows):
    n_rows, d = xs.shape
    f = w1.shape[2]
    assert f % FF_CHUNK == 0
    n_blocks = n_rows // EXPERT_BLOCK
    n_steps = block_e.shape[0]
    grid_spec = pltpu.PrefetchScalarGridSpec(
        num_scalar_prefetch=2,
        grid=(n_steps,),
        in_specs=[pl.BlockSpec((EXPERT_BLOCK, d), lambda j, be, bv: (jnp.minimum(j, n_blocks - 1), 0)),
                  pl.BlockSpec((None, d, f), lambda j, be, bv: (be[j], 0, 0)),
                  pl.BlockSpec((None, d, f), lambda j, be, bv: (be[j], 0, 0)),
                  pl.BlockSpec((None, f, d), lambda j, be, bv: (be[j], 0, 0))],
        out_specs=pl.BlockSpec((EXPERT_BLOCK, d), lambda j, be, bv: (j, 0)),
    )
    return pl.pallas_call(
        _expert_kernel,
        grid_spec=grid_spec,
        out_shape=jax.ShapeDtypeStruct((n_steps * EXPERT_BLOCK, d), BF16),
        compiler_params=_cparams(("arbitrary",)),
        name="moe_experts",
    )(block_e, block_rows, xs, w1, w3, w2)


def _combine_kernel(run_ref, x_ref, cm_ref, y_hbm, g_ref, beta_ref, o_ref, win, sem, *, alpha):
    i = pl.program_id(0)
    n_steps = pl.num_programs(0)
    tm = x_ref.shape[0]
    max_run = tm + COMB_ALIGN
    sizes = [COMB_ALIGN << b for b in range((max_run // COMB_ALIGN).bit_length())]

    def run_copies(step, slot, fn):
        for e in range(N_EXPERTS):
            length = run_ref[step * 3 * N_EXPERTS + e]
            src = run_ref[step * 3 * N_EXPERTS + N_EXPERTS + e]
            dst = run_ref[step * 3 * N_EXPERTS + 2 * N_EXPERTS + e]
            done = 0 * length
            for size in sizes:
                copy = pltpu.make_async_copy(
                    y_hbm.at[pl.ds(pl.multiple_of(src + done, COMB_ALIGN), size), :],
                    win.at[slot, pl.ds(pl.multiple_of(dst + done, COMB_ALIGN), size), :],
                    sem.at[slot, e])

                @pl.when((length & size) != 0)
                def _():
                    fn(copy)

                done = done + (length & size)

    @pl.when(i == 0)
    def _():
        win[...] = jnp.zeros_like(win)
        run_copies(0, 0, lambda c: c.start())

    slot = i % 2

    @pl.when(i + 1 < n_steps)
    def _():
        run_copies(i + 1, 1 - slot, lambda c: c.start())

    run_copies(i, slot, lambda c: c.wait())

    cm = cm_ref[...]
    col = lax.broadcasted_iota(I32, (tm, win.shape[1]), 1).astype(F32)
    p1 = jnp.where(col == cm[:, 0:1], 1.0, 0.0).astype(BF16)
    p2 = jnp.where(col == cm[:, 1:2], 1.0, 0.0).astype(BF16)
    rows = _dot(jnp.concatenate([p1, p2], axis=0), win[slot])
    y = cm[:, 2:3] * rows[:tm] + cm[:, 3:4] * rows[tm:]
    o_ref[...] = _layer_norm(alpha * x_ref[...] + y, g_ref[...], beta_ref[...])


def _combine_ln(x, cm, runs, yb, g, beta, alpha):
    n, d = x.shape
    tm = min(COMB_TILE, n)
    packed = TOP_K * tm + N_EXPERTS * 2 * COMB_ALIGN
    grid_spec = pltpu.PrefetchScalarGridSpec(
        num_scalar_prefetch=1,
        grid=(n // tm,),
        in_specs=[pl.BlockSpec((tm, d), lambda i, rr: (i, 0)),
                  pl.BlockSpec((tm, 8), lambda i, rr: (i, 0)),
                  pl.BlockSpec(memory_space=pl.ANY),
                  pl.BlockSpec((1, d), lambda i, rr: (0, 0)),
                  pl.BlockSpec((1, d), lambda i, rr: (0, 0))],
        out_specs=pl.BlockSpec((tm, d), lambda i, rr: (i, 0)),
        scratch_shapes=[pltpu.VMEM((2, packed, d), BF16),
                        pltpu.SemaphoreType.DMA((2, N_EXPERTS))],
    )
    return pl.pallas_call(
        functools.partial(_combine_kernel, alpha=alpha),
        grid_spec=grid_spec,
        out_shape=jax.ShapeDtypeStruct((n, d), F32),
        compiler_params=_cparams(("arbitrary",)),
        name="moe_combine_ln",
    )(runs, x, cm, yb, g[None, :], beta[None, :])


def _moe_ln(x, router_w, router_b, w1, w3, w2, g, beta, alpha):
    n, d = x.shape
    n_exp = router_w.shape[1]
    assert n_exp == N_EXPERTS
    pad = ROUTER_ROWS - n_exp
    w_t = jnp.pad(router_w.T, ((0, pad), (0, 0)))
    w_hi = w_t.astype(BF16)
    w_lo = (w_t - w_hi.astype(F32)).astype(BF16)
    b_col = jnp.broadcast_to(jnp.pad(router_b, (0, pad), constant_values=NEG_BIG)[:, None], (ROUTER_ROWS, HEAD_W))
    meta, xb = _router(x, w_hi, w_lo, b_col)

    e1, e2 = meta[0].astype(I32), meta[1].astype(I32)
    r1, r2 = meta[2].astype(I32), meta[3].astype(I32)
    experts = jnp.arange(n_exp, dtype=I32)[None, :]
    pick1 = (e1[:, None] == experts).astype(I32)
    pick2 = (e2[:, None] == experts).astype(I32)

    ct = min(COMB_TILE, n)
    ts = min(SRC_TILE, n)
    n_src = n // ts
    per_ct = jnp.sum((pick1 + pick2).reshape(n // ct, ct, n_exp), axis=1)
    before_ct = jnp.cumsum(per_ct, axis=0) - per_ct
    per_src = jnp.sum(per_ct.reshape(n_src, ts // ct, n_exp), axis=1)
    before_src = before_ct[:: ts // ct]
    seg_len = (per_src + SEG_ALIGN - 1) // SEG_ALIGN * SEG_ALIGN
    local_off = jnp.cumsum(seg_len, axis=1) - seg_len
    rows_e = jnp.sum(seg_len, axis=0)
    region = (rows_e + EXPERT_BLOCK - 1) // EXPERT_BLOCK * EXPERT_BLOCK
    region_end = jnp.cumsum(region)
    region_start = region_end - region
    seg_start = region_start[None, :] + jnp.cumsum(seg_len, axis=0) - seg_len
    n_blocks = (n * TOP_K + n_src * n_exp * (SEG_ALIGN - 1)) // EXPERT_BLOCK + 1 + n_exp
    n_rows = n_blocks * EXPERT_BLOCK

    def per_token(table, pick):
        tiles = table.shape[0]
        tok = jnp.broadcast_to(table[:, None, :], (tiles, n // tiles, n_exp)).reshape(n, n_exp)
        return jnp.sum(pick * tok, axis=1)

    in_tile1 = r1 - per_token(before_src, pick1)
    in_tile2 = r2 - per_token(before_src, pick2)
    pos1 = per_token(local_off, pick1) + in_tile1
    pos2 = per_token(local_off, pick2) + in_tile2
    dest1 = per_token(seg_start, pick1) + in_tile1
    dest2 = per_token(seg_start, pick2) + in_tile2
    seg = jnp.concatenate([seg_len, local_off, seg_start], axis=1).reshape(-1).astype(I32)
    xs = _dispatch(xb, pos1, pos2, seg, n_rows)

    bstart = jnp.arange(n_blocks + 1, dtype=I32) * EXPERT_BLOCK
    block_e = jnp.minimum(jnp.sum((bstart[:, None] >= region_end[None, :]).astype(I32), axis=1), n_exp - 1)
    block_rows = jnp.clip((region_start + rows_e)[block_e] - bstart, 0, EXPERT_BLOCK).astype(I32)
    yb = _experts(xs, w1, w3, w2, block_e, block_rows)

    first_ct = jnp.repeat(seg_start - before_src, ts // ct, axis=0) + before_ct
    run_start = first_ct // COMB_ALIGN * COMB_ALIGN
    run_len = (first_ct + per_ct - run_start + COMB_ALIGN - 1) // COMB_ALIGN * COMB_ALIGN
    run_len = jnp.where(per_ct > 0, run_len, 0)
    run_off = jnp.cumsum(run_len, axis=1) - run_len
    c1 = per_token(run_off - run_start, pick1) + dest1
    c2 = per_token(run_off - run_start, pick2) + dest2
    runs = jnp.concatenate([run_len, run_start, run_off], axis=1).reshape(-1).astype(I32)
    cm = jnp.stack([c1.astype(F32), c2.astype(F32), meta[4], meta[5]]
                   + [jnp.zeros((n,), F32)] * 4, axis=1)
    return _combine_ln(x, cm, runs, yb, g, beta, alpha)


def kernel(x, mem, ln1_g, ln1_b, lnx_g, lnx_b, ln2_g, ln2_b, xa_wq, xa_wkv, xa_wo, attn_w_in, attn_w_o, diff_lq1, diff_lk1, diff_lq2, diff_lk2, diff_subln_g, ffn_w1, ffn_w3, ffn_w2, conv_w_pw1, conv_b_pw1, conv_dw_w, conv_dw_b, conv_ln_g, conv_ln_b, conv_w_pw2, conv_b_pw2, moe_router_w, moe_router_b, moe_w1, moe_w3, moe_w2):
    batch, seq, d = x.shape
    n_mem = mem.shape[1]
    depth = ln1_g.shape[0]
    alpha = (2 * depth) ** 0.25
    sb_width = SB_HEADS * HEAD_W

    bf = lambda w: w.astype(BF16)
    cos_t, sa_t, sb_t = _rope_tables(seq)
    h = x.reshape(batch * seq, d)
    mem2 = mem.reshape(batch * n_mem, d)

    for i in range(depth):
        j = i // 2
        if i % 2 == 0:
            lambda_init = 0.8 - 0.6 * math.exp(-0.3 * i)
            qkv = _inproj(h, bf(attn_w_in[j]), cos_t, sa_t, sb_t, seq)
            o_sb = _sb_attention(qkv, batch, seq)
            o_df = _diff_attention(qkv, diff_lq1[j], diff_lk1[j], diff_lq2[j], diff_lk2[j],
                                   diff_subln_g[j], batch, seq, lambda_init)
            w_o = bf(attn_w_o[j])
            h = _proj2_ln(o_sb, o_df, w_o[:sb_width], w_o[sb_width:], h, ln1_g[i], ln1_b[i], alpha)
        else:
            h = _conv_ln(h, bf(conv_w_pw1[j]), conv_b_pw1[j], conv_dw_w[j], conv_dw_b[j],
                         conv_ln_g[j], conv_ln_b[j], bf(conv_w_pw2[j]), conv_b_pw2[j],
                         ln1_g[i], ln1_b[i], seq, alpha)
        kv = _matmul(mem2, bf(xa_wkv[i]), "xattn_kv")
        h = _xattn_ln(h, kv, bf(xa_wq[i]), bf(xa_wo[i]), lnx_g[i], lnx_b[i], seq, n_mem, alpha)
        if i % 2 == 0:
            h = _ffn_ln(h, bf(ffn_w1[j]), bf(ffn_w3[j]), bf(ffn_w2[j]), ln2_g[i], ln2_b[i], alpha)
        else:
            h = _moe_ln(h, moe_router_w[j], moe_router_b[j], bf(moe_w1[j]), bf(moe_w3[j]), bf(moe_w2[j]),
                        ln2_g[i], ln2_b[i], alpha)
    return h.reshape(batch, seq, d)
```
